```python
import jax, jax.numpy as jnp
from jax import lax
import numpy as np

D_MODEL = 2048
BATCH = 8
SEQ = 2048
DEPTH = 1
DEC_BATCH = 8
DEC_SEQ = 16
PAST_LEN = 4096

CHUNK = 64
Q_BLOCK = 128
EPS = 1e-6
GMLP_CHUNK = 128
GMLP_DIM = D_MODEL // 4
GMLP_GROUPS = 4
GMLP_GROUP_DIM = GMLP_DIM // GMLP_GROUPS
V_HEAD = 128
QK_NOPE = 128
QK_ROPE = 64
MLA_HEADS = D_MODEL // (2 * V_HEAD)
MLA_DIM = MLA_HEADS * V_HEAD
Q_LORA = D_MODEL // 4
KV_LORA = D_MODEL // 8
ROPE_THETA = 10000.0
MLA_SCALE = (QK_NOPE + QK_ROPE) ** -0.5
MEM_TOKENS = 256
MEM_HEADS = 4
MEM_DIM = D_MODEL // 4
MEM_HEAD_DIM = MEM_DIM // MEM_HEADS
MEM_SCALE = MEM_HEAD_DIM ** -0.5
MIX_DIM = GMLP_DIM + MLA_DIM + MEM_DIM
OFF_Q = 2 * GMLP_DIM
OFF_KV = OFF_Q + Q_LORA
OFF_KR = OFF_KV + KV_LORA
OFF_M = OFF_KR + QK_ROPE
IN_DIM = OFF_M + MEM_DIM
PEER_HEADS = 8
N_KEYS = 128
N_EXPERTS = N_KEYS * N_KEYS
PEER_QDIM = 256
PEER_HALF = PEER_QDIM // 2
PEER_TOPK = 16
PEER_BLOCK = 128

kernel_name = 'hybrid_gmlp_mla_peer_stream_step'


def rmsnorm(x, g):
    xf = x.astype(jnp.float32)
    y = xf * lax.rsqrt(jnp.mean(xf * xf, -1, keepdims=True) + EPS)
    return (y * g.astype(jnp.float32)).astype(x.dtype)


def layernorm(x, g, b):
    xf = x.astype(jnp.float32)
    mu = jnp.mean(xf, -1, keepdims=True)
    var = jnp.mean(jnp.square(xf - mu), -1, keepdims=True)
    y = (xf - mu) * lax.rsqrt(var + EPS)
    return (y * g.astype(jnp.float32) + b.astype(jnp.float32)).astype(x.dtype)


def rope_tables(pos, dtype):
    half = QK_ROPE // 2
    inv = ROPE_THETA ** (-jnp.arange(half, dtype=jnp.float32) / half)
    ang = pos.astype(jnp.float32)[:, None] * inv[None, :]
    return jnp.cos(ang).astype(dtype), jnp.sin(ang).astype(dtype)


def apply_rope(x, cos, sin):
    x1, x2 = jnp.split(x, 2, axis=-1)
    return jnp.concatenate([x1 * cos - x2 * sin, x1 * sin + x2 * cos], -1)


def split_in(proj):
    return (proj[..., :OFF_Q], proj[..., OFF_Q:OFF_KV], proj[..., OFF_KV:OFF_KR],
            proj[..., OFF_KR:OFF_M], proj[..., OFF_M:])


def gmlp_uv(uv, g_v, b_v):
    uv = jax.nn.gelu(uv)
    return uv[..., :GMLP_DIM], layernorm(uv[..., GMLP_DIM:], g_v, b_v)


def spatial_gate(u, v, w_s, b_s):
    L = v.shape[-2]
    w = jnp.tril(w_s[:, :L, :L])
    vg = v.reshape(v.shape[:-1] + (GMLP_GROUPS, GMLP_GROUP_DIM))
    z = jnp.einsum('gts,bnsgc->bntgc', w, vg) + b_s[:, :L].T[None, None, :, :, None]
    return u * z.reshape(u.shape)


def mla_queries(c_q, pos, g_q, w_uq, w_uk):
    cq = rmsnorm(c_q, g_q)
    q = (cq @ w_uq).reshape(cq.shape[:-1] + (MLA_HEADS, QK_NOPE + QK_ROPE))
    cos, sin = rope_tables(pos, q.dtype)
    q_rope = apply_rope(q[..., QK_NOPE:], cos[:, None, :], sin[:, None, :])
    q_lat = jnp.einsum('bthn,chn->bthc', q[..., :QK_NOPE], w_uk)
    return q_lat, q_rope


def mla_latent(c_kv, k_r, pos, g_kv):
    ckv = rmsnorm(c_kv, g_kv)
    cos, sin = rope_tables(pos, k_r.dtype)
    return ckv, apply_rope(k_r, cos, sin)


def mla_attend(q_lat, q_rope, ckv, krope, mask):
    s = (jnp.einsum('bthc,bsc->bhts', q_lat, ckv)
         + jnp.einsum('bthr,bsr->bhts', q_rope, krope)).astype(jnp.float32) * MLA_SCALE
    if mask is not None:
        s = jnp.where(mask, s, -jnp.inf)
    p = jax.nn.softmax(s, axis=-1).astype(ckv.dtype)
    return jnp.einsum('bhts,bsc->bthc', p, ckv)


def mla_out(o_lat, w_uv):
    o = jnp.einsum('bthc,chv->bthv', o_lat, w_uv)
    return o.reshape(o.shape[:2] + (MLA_DIM,))


def mem_kv(mem, g_mem, w_mk, w_mv):
    mn = rmsnorm(mem, g_mem)
    shp = mem.shape[:2] + (MEM_HEADS, MEM_HEAD_DIM)
    return (mn @ w_mk).reshape(shp), (mn @ w_mv).reshape(shp)


def mem_attend(q_m, mk, mv):
    q = q_m.reshape(q_m.shape[:2] + (MEM_HEADS, MEM_HEAD_DIM))
    s = jnp.einsum('bthd,bmhd->bhtm', q, mk).astype(jnp.float32) * MEM_SCALE
    p = jax.nn.softmax(s, axis=-1).astype(mv.dtype)
    o = jnp.einsum('bhtm,bmhd->bthd', p, mv)
    return o.reshape(o.shape[:2] + (MEM_DIM,))


def merge_heads(y_a, y_b, y_m, lp):
    y = jnp.concatenate([rmsnorm(y_a, lp['g_out_a']), rmsnorm(y_b, lp['g_out_b']),
                         rmsnorm(y_m, lp['g_out_m'])], axis=-1)
    return y @ lp['w_out']


def peer_ffn(x, w_pq, sk1, sk2, pu, pv):
    T = x.shape[0]
    q = (x @ w_pq).reshape(T, PEER_HEADS, PEER_QDIM)
    s1 = jnp.einsum('thd,kd->thk', q[..., :PEER_HALF], sk1).astype(jnp.float32)
    s2 = jnp.einsum('thd,kd->thk', q[..., PEER_HALF:], sk2).astype(jnp.float32)
    v1, i1 = lax.top_k(s1, PEER_TOPK)
    v2, i2 = lax.top_k(s2, PEER_TOPK)
    cand = (v1[..., :, None] + v2[..., None, :]).reshape(T, PEER_HEADS, PEER_TOPK * PEER_TOPK)
    vals, ci = lax.top_k(cand, PEER_TOPK)
    e1 = jnp.take_along_axis(i1, ci // PEER_TOPK, axis=-1)
    e2 = jnp.take_along_axis(i2, ci % PEER_TOPK, axis=-1)
    eid = e1 * N_KEYS + e2
    g = jax.nn.softmax(vals, axis=-1)
    h = jnp.einsum('td,thkd->thk', x, pu[eid])
    w = (g * jax.nn.gelu(h.astype(jnp.float32))).astype(x.dtype)
    return jnp.einsum('thk,thkd->td', w, pv[eid])


def prompt_layer(x, mem, lp):
    B, S, _ = x.shape
    pos = jnp.arange(S)
    xn = rmsnorm(x, lp['g_attn'])
    uv, c_q, c_kv, k_r, q_m = split_in(xn @ lp['w_in'])
    u, v = gmlp_uv(uv, lp['g_v'], lp['b_v'])
    n = S // GMLP_CHUNK
    y_a = spatial_gate(u.reshape(B, n, GMLP_CHUNK, GMLP_DIM), v.reshape(B, n, GMLP_CHUNK, GMLP_DIM),
                       lp['w_s'], lp['b_s']).reshape(B, S, GMLP_DIM)
    q_lat, q_rope = mla_queries(c_q, pos, lp['g_q'], lp['w_uq'], lp['w_uk'])
    ckv, krope = mla_latent(c_kv, k_r, pos, lp['g_kv'])
    nb = S // Q_BLOCK
    ql = q_lat.reshape(B, nb, Q_BLOCK, MLA_HEADS, KV_LORA).swapaxes(0, 1)
    qr = q_rope.reshape(B, nb, Q_BLOCK, MLA_HEADS, QK_ROPE).swapaxes(0, 1)
    k_chunk = pos // CHUNK

    def attend_block(args):
        i, ql_b, qr_b = args
        q_chunk = (i * Q_BLOCK + jnp.arange(Q_BLOCK)) // CHUNK
        mask = k_chunk[None, :] <= q_chunk[:, None]
        return mla_attend(ql_b, qr_b, ckv, krope, mask[None, None])

    o_lat = lax.map(attend_block, (jnp.arange(nb), ql, qr))
    o_lat = o_lat.swapaxes(0, 1).reshape(B, S, MLA_HEADS, KV_LORA)
    y_b = mla_out(o_lat, lp['w_uv'])
    mk, mv = mem_kv(mem, lp['g_mem'], lp['w_mk'], lp['w_mv'])
    y_m = mem_attend(q_m, mk, mv)
    x = x + merge_heads(y_a, y_b, y_m, lp)
    xf = rmsnorm(x, lp['g_ffn']).reshape(B * S // PEER_BLOCK, PEER_BLOCK, D_MODEL)
    y_f = lax.map(lambda xb: peer_ffn(xb, lp['w_pq'], lp['sub_keys1'], lp['sub_keys2'],
                                      lp['peer_u'], lp['peer_v']), xf)
    x = x + y_f.reshape(B, S, D_MODEL)
    return x, ckv, krope, mk, mv


def sample_layer(x, cache_ckv, cache_krope, mem_k, mem_v, lp):
    B, T, _ = x.shape
    past = cache_ckv.shape[1]
    pos = past + jnp.arange(T)
    xn = rmsnorm(x, lp['g_attn'])
    uv, c_q, c_kv, k_r, q_m = split_in(xn @ lp['w_in'])
    u, v = gmlp_uv(uv, lp['g_v'], lp['b_v'])
    y_a = spatial_gate(u[:, None], v[:, None], lp['w_s'], lp['b_s'])[:, 0]
    q_lat, q_rope = mla_queries(c_q, pos, lp['g_q'], lp['w_uq'], lp['w_uk'])
    ckv_new, krope_new = mla_latent(c_kv, k_r, pos, lp['g_kv'])
    ckv_all = jnp.concatenate([cache_ckv, ckv_new], axis=1)
    krope_all = jnp.concatenate([cache_krope, krope_new], axis=1)
    y_b = mla_out(mla_attend(q_lat, q_rope, ckv_all, krope_all, None), lp['w_uv'])
    y_m = mem_attend(q_m, mem_k, mem_v)
    x = x + merge_heads(y_a, y_b, y_m, lp)
    xf = rmsnorm(x, lp['g_ffn']).reshape(B * T, D_MODEL)
    y_f = peer_ffn(xf, lp['w_pq'], lp['sub_keys1'], lp['sub_keys2'], lp['peer_u'], lp['peer_v'])
    x = x + y_f.reshape(B, T, D_MODEL)
    return x, ckv_new, krope_new, v


def setup_inputs(seed: int = 0) -> dict:
    key = jax.random.key(seed)
    ks = jax.random.split(key, 40)
    f32 = jnp.float32

    def nrm(k, shape, scale):
        return jax.random.normal(k, shape, f32) * scale

    def gain(k, shape):
        return 1.0 + 0.01 * jax.random.normal(k, shape, f32)

    L = DEPTH
    return {
        'x_prompt': nrm(ks[0], (BATCH, SEQ, D_MODEL), 1.0),
        'x_sample': nrm(ks[1], (DEC_BATCH, DEC_SEQ, D_MODEL), 1.0),
        'cache_mla_ckv': nrm(ks[2], (L, DEC_BATCH, PAST_LEN, KV_LORA), 1.0),
        'cache_mla_krope': nrm(ks[3], (L, DEC_BATCH, PAST_LEN, QK_ROPE), 1.0),
        'cache_mem_k': nrm(ks[4], (L, DEC_BATCH, MEM_TOKENS, MEM_HEADS, MEM_HEAD_DIM), 1.0),
        'cache_mem_v': nrm(ks[5], (L, DEC_BATCH, MEM_TOKENS, MEM_HEADS, MEM_HEAD_DIM), 1.0),
        'mem_prompt': nrm(ks[6], (BATCH, MEM_TOKENS, D_MODEL), 1.0),
        'g_attn': gain(ks[7], (L, D_MODEL)),
        'w_in': nrm(ks[8], (L, D_MODEL, IN_DIM), D_MODEL ** -0.5),
        'g_v': gain(ks[9], (L, GMLP_DIM)),
        'b_v': nrm(ks[10], (L, GMLP_DIM), 0.01),
        'w_s': nrm(ks[11], (L, GMLP_GROUPS, GMLP_CHUNK, GMLP_CHUNK), GMLP_CHUNK ** -0.5),
        'b_s': gain(ks[12], (L, GMLP_GROUPS, GMLP_CHUNK)),
        'g_q': gain(ks[13], (L, Q_LORA)),
        'w_uq': nrm(ks[14], (L, Q_LORA, MLA_HEADS * (QK_NOPE + QK_ROPE)), Q_LORA ** -0.5),
        'w_uk': nrm(ks[15], (L, KV_LORA, MLA_HEADS, QK_NOPE), KV_LORA ** -0.5),
        'w_uv': nrm(ks[16], (L, KV_LORA, MLA_HEADS, V_HEAD), KV_LORA ** -0.5),
        'g_kv': gain(ks[17], (L, KV_LORA)),
        'g_mem': gain(ks[18], (L, D_MODEL)),
        'w_mk': nrm(ks[19], (L, D_MODEL, MEM_DIM), D_MODEL ** -0.5),
        'w_mv': nrm(ks[20], (L, D_MODEL, MEM_DIM), D_MODEL ** -0.5),
        'g_out_a': gain(ks[21], (L, GMLP_DIM)),
        'g_out_b': gain(ks[22], (L, MLA_DIM)),
        'g_out_m': gain(ks[23], (L, MEM_DIM)),
        'w_out': nrm(ks[24], (L, MIX_DIM, D_MODEL), MIX_DIM ** -0.5),
        'g_ffn': gain(ks[25], (L, D_MODEL)),
        'w_pq': nrm(ks[26], (L, D_MODEL, PEER_HEADS * PEER_QDIM), D_MODEL ** -0.5),
        'sub_keys1': nrm(ks[27], (L, N_KEYS, PEER_HALF), PEER_HALF ** -0.5),
        'sub_keys2': nrm(ks[28], (L, N_KEYS, PEER_HALF), PEER_HALF ** -0.5),
        'peer_u': nrm(ks[29], (L, N_EXPERTS, D_MODEL), D_MODEL ** -0.5),
        'peer_v': nrm(ks[30], (L, N_EXPERTS, D_MODEL), 0.5),
        'g_final': gain(ks[31], (D_MODEL,)),
    }


def reference(x_prompt, x_sample, cache_mla_ckv, cache_mla_krope, cache_mem_k, cache_mem_v,
              mem_prompt, g_attn, w_in, g_v, b_v, w_s, b_s, g_q, w_uq, w_uk, w_uv, g_kv,
              g_mem, w_mk, w_mv, g_out_a, g_out_b, g_out_m, w_out, g_ffn, w_pq,
              sub_keys1, sub_keys2, peer_u, peer_v, g_final):
    yp, ys = x_prompt, x_sample
    ckv_p, kr_p, mk_p, mv_p, ckv_s, kr_s, gv_s = [], [], [], [], [], [], []
    for l in range(DEPTH):
        lp = {'g_attn': g_attn[l], 'w_in': w_in[l], 'g_v': g_v[l], 'b_v': b_v[l],
              'w_s': w_s[l], 'b_s': b_s[l], 'g_q': g_q[l], 'w_uq': w_uq[l], 'w_uk': w_uk[l],
              'w_uv': w_uv[l], 'g_kv': g_kv[l], 'g_mem': g_mem[l], 'w_mk': w_mk[l],
              'w_mv': w_mv[l], 'g_out_a': g_out_a[l], 'g_out_b': g_out_b[l],
              'g_out_m': g_out_m[l], 'w_out': w_out[l], 'g_ffn': g_ffn[l], 'w_pq': w_pq[l],
              'sub_keys1': sub_keys1[l], 'sub_keys2': sub_keys2[l],
              'peer_u': peer_u[l], 'peer_v': peer_v[l]}
        yp, c, r, k, v = prompt_layer(yp, mem_prompt, lp)
        ckv_p.append(c)
        kr_p.append(r)
        mk_p.append(k)
        mv_p.append(v)
        ys, c2, r2, gv = sample_layer(ys, cache_mla_ckv[l], cache_mla_krope[l],
                                      cache_mem_k[l], cache_mem_v[l], lp)
        ckv_s.append(c2)
        kr_s.append(r2)
        gv_s.append(gv)
    y_prompt = rmsnorm(yp, g_final)
    y_sample = rmsnorm(ys, g_final)
    return (y_prompt, y_sample, jnp.stack(ckv_p), jnp.stack(kr_p), jnp.stack(mk_p),
            jnp.stack(mv_p), jnp.stack(ckv_s), jnp.stack(kr_s), jnp.stack(gv_s))
```

```python
import functools
import math

import jax
import jax.numpy as jnp
from jax import lax
from jax.experimental import pallas as pl
from jax.experimental.pallas import tpu as pltpu

F32 = jnp.float32
BF16 = jnp.bfloat16
I32 = jnp.int32
U32 = jnp.uint32

D_MODEL = 2048
CHUNK = 64
CHUNK_SHIFT = 6
EPS = 1e-6
GMLP_CHUNK = 128
GMLP_DIM = 512
GMLP_GROUPS = 4
GMLP_GROUP_DIM = 128
V_HEAD = 128
QK_NOPE = 128
QK_ROPE = 64
MLA_HEADS = 8
MLA_DIM = 1024
Q_LORA = 512
KV_LORA = 256
ROPE_THETA = 10000.0
MLA_SCALE = (QK_NOPE + QK_ROPE) ** -0.5
QCAT = KV_LORA + QK_ROPE
MEM_TOKENS = 256
MEM_HEADS = 4
MEM_DIM = 512
MEM_HEAD_DIM = 128
MEM_SCALE = MEM_HEAD_DIM ** -0.5
PEER_HEADS = 8
N_KEYS = 128
PEER_QDIM = 256
PEER_HALF = 128
PEER_TOPK = 16
PEER_PAIRS = PEER_HEADS * PEER_TOPK
HALF_D = D_MODEL // 2
LANES = 128
ROW_CHUNKS = 2 * HALF_D // LANES

_C_UV, _C_Q, _C_KV, _C_KR, _C_KROT, _C_M, _C_END = 0, 1024, 1536, 1792, 1856, 1920, 2432

VMEM_LIMIT = 48 * 1024 * 1024


def _cparams(n_grid):
    return pltpu.CompilerParams(dimension_semantics=("arbitrary",) * n_grid,
                                vmem_limit_bytes=VMEM_LIMIT)


def _rms(x, g):
    return x * lax.rsqrt(jnp.mean(x * x, axis=-1, keepdims=True) + EPS) * g


def _gelu(x):
    return x * (0.5 * (1.0 + jnp.tanh(math.sqrt(2.0 / math.pi) * (x + 0.044715 * (x * x * x)))))


def _dot(a, b):
    return jnp.dot(a, b, preferred_element_type=F32)


def _dot_nt(a, b):
    return lax.dot_general(a, b, (((1,), (1,)), ((), ())), preferred_element_type=F32)


def _full(shape):
    n = len(shape)
    return pl.BlockSpec(shape, lambda *_: (0,) * n)


def _proj_kernel(x_ref, g_ref, w_ref, gkv_ref, cos_ref, sin_ref,
                 uv_ref, cq_ref, ckv_ref, kr_ref, kcat_ref, qm_ref):
    xn = _rms(x_ref[...], g_ref[...])
    proj = _dot(xn.astype(BF16), w_ref[...])
    uv_ref[...] = proj[:, _C_UV:_C_Q]
    cq_ref[...] = proj[:, _C_Q:_C_KV]
    ckv = _rms(proj[:, _C_KV:_C_KR], gkv_ref[...])
    ckv_ref[...] = ckv
    krope = proj[:, _C_KR:_C_KROT] * cos_ref[...] + proj[:, _C_KROT:_C_M] * sin_ref[...]
    kr_ref[...] = krope
    kcat_ref[:, :KV_LORA] = ckv.astype(BF16)
    kcat_ref[:, KV_LORA:] = krope.astype(BF16)
    qm_ref[...] = proj[:, _C_M:_C_END]


def _proj(x, g_attn, w_ext, g_kv, cos2, sin2, tm):
    n = x.shape[0]
    nper = cos2.shape[0] // tm
    row = lambda w: pl.BlockSpec((tm, w), lambda i: (i, 0))
    tab = pl.BlockSpec((tm, QK_ROPE), lambda i: (i % nper, 0))
    return pl.pallas_call(
        _proj_kernel,
        grid=(n // tm,),
        in_specs=[row(D_MODEL), _full((1, D_MODEL)), _full(w_ext.shape), _full((1, KV_LORA)), tab, tab],
        out_specs=[row(2 * GMLP_DIM), row(Q_LORA), row(KV_LORA), row(QK_ROPE), row(QCAT), row(MEM_DIM)],
        out_shape=[jax.ShapeDtypeStruct((n, 2 * GMLP_DIM), F32), jax.ShapeDtypeStruct((n, Q_LORA), F32),
                   jax.ShapeDtypeStruct((n, KV_LORA), F32), jax.ShapeDtypeStruct((n, QK_ROPE), F32),
                   jax.ShapeDtypeStruct((n, QCAT), BF16), jax.ShapeDtypeStruct((n, MEM_DIM), F32)],
        compiler_params=_cparams(1), name="proj",
    )(x, g_attn, w_ext, g_kv, cos2, sin2)


def _gmlp_kernel(uv_ref, gv_ref, bv_ref, ws_ref, bst_ref, goa_ref, ya_ref, v_ref, *, rows):
    uv = _gelu(uv_ref[...])
    u = uv[:, :GMLP_DIM]
    vr = uv[:, GMLP_DIM:]
    mu = jnp.mean(vr, axis=-1, keepdims=True)
    var = jnp.mean(jnp.square(vr - mu), axis=-1, keepdims=True)
    v = (vr - mu) * lax.rsqrt(var + EPS) * gv_ref[...] + bv_ref[...]
    v_ref[...] = v
    r = lax.broadcasted_iota(I32, (rows, rows), 0)
    c = lax.broadcasted_iota(I32, (rows, rows), 1)
    zs = []
    for g in range(GMLP_GROUPS):
        w = jnp.where(r >= c, ws_ref[g, :rows, :rows], 0.0).astype(BF16)
        vg = v[:, g * GMLP_GROUP_DIM:(g + 1) * GMLP_GROUP_DIM].astype(BF16)
        zs.append(_dot(w, vg) + bst_ref[:rows, g:g + 1])
    y = u * jnp.concatenate(zs, axis=-1)
    ya_ref[...] = _rms(y, goa_ref[...]).astype(BF16)


def _gmlp(uv, g_v, b_v, w_s, b_st, g_out_a, rows):
    n = uv.shape[0]
    row = lambda w: pl.BlockSpec((rows, w), lambda i: (i, 0))
    return pl.pallas_call(
        functools.partial(_gmlp_kernel, rows=rows),
        grid=(n // rows,),
        in_specs=[row(2 * GMLP_DIM), _full((1, GMLP_DIM)), _full((1, GMLP_DIM)), _full(w_s.shape),
                  _full(b_st.shape), _full((1, GMLP_DIM))],
        out_specs=[row(GMLP_DIM), row(GMLP_DIM)],
        out_shape=[jax.ShapeDtypeStruct((n, GMLP_DIM), BF16), jax.ShapeDtypeStruct((n, GMLP_DIM), F32)],
        compiler_params=_cparams(1), name="gmlp",
    )(uv, g_v, b_v, w_s, b_st, g_out_a)


def _mlaq_kernel(cq_ref, gq_ref, wq_ref, wuk_ref, cos_ref, sin_ref, q_ref):
    cq = _rms(cq_ref[...], gq_ref[...])
    q = _dot(cq.astype(BF16), wq_ref[...])
    nr = MLA_HEADS * QK_NOPE
    rw = MLA_HEADS * QK_ROPE
    qrope = q[:, nr:nr + rw] * cos_ref[...] + q[:, nr + rw:] * sin_ref[...]
    for h in range(MLA_HEADS):
        qn = q[:, h * QK_NOPE:(h + 1) * QK_NOPE].astype(BF16)
        q_ref[h, :, :KV_LORA] = _dot(qn, wuk_ref[h]).astype(BF16)
        q_ref[h, :, KV_LORA:] = qrope[:, h * QK_ROPE:(h + 1) * QK_ROPE].astype(BF16)


def _mlaq(c_q, g_q, w_uqp, w_ukt, cos8, sin8, tm):
    n = c_q.shape[0]
    nper = cos8.shape[0] // tm
    tab = pl.BlockSpec((tm, MLA_HEADS * QK_ROPE), lambda i: (i % nper, 0))
    return pl.pallas_call(
        _mlaq_kernel,
        grid=(n // tm,),
        in_specs=[pl.BlockSpec((tm, Q_LORA), lambda i: (i, 0)), _full((1, Q_LORA)), _full(w_uqp.shape),
                  _full(w_ukt.shape), tab, tab],
        out_specs=pl.BlockSpec((MLA_HEADS, tm, QCAT), lambda i: (0, i, 0)),
        out_shape=jax.ShapeDtypeStruct((MLA_HEADS, n, QCAT), BF16),
        compiler_params=_cparams(1), name="mla_q",
    )(c_q, g_q, w_uqp, w_ukt, cos8, sin8)


def _mla_finish(o, wuv_ref, gob_ref, tq):
    ys = [_dot(o[h * tq:(h + 1) * tq].astype(BF16), wuv_ref[h]) for h in range(MLA_HEADS)]
    return _rms(jnp.concatenate(ys, axis=-1), gob_ref[...]).astype(BF16)


def _mla_attn_kernel(q_ref, k_ref, wuv_ref, gob_ref, yb_ref, m_ref, l_ref, acc_ref, *, tq):
    i = pl.program_id(1)
    rows = MLA_HEADS * tq
    q = q_ref[...].reshape(rows, QCAT)
    m_ref[...] = jnp.full((rows, 1), -jnp.inf, F32)
    l_ref[...] = jnp.zeros((rows, 1), F32)
    acc_ref[...] = jnp.zeros((rows, KV_LORA), F32)
    qpos = i * tq + lax.broadcasted_iota(I32, (MLA_HEADS, tq, tq), 1).reshape(rows, tq)
    kofs = lax.broadcasted_iota(I32, (rows, tq), 1)

    def body(j, carry):
        k = k_ref[pl.ds(pl.multiple_of(j * tq, tq), tq), :]
        s = _dot_nt(q, k) * MLA_SCALE
        s = jnp.where(((j * tq + kofs) >> CHUNK_SHIFT) <= (qpos >> CHUNK_SHIFT), s, -jnp.inf)
        m_old = m_ref[...]
        m_new = jnp.maximum(m_old, jnp.max(s, axis=-1, keepdims=True))
        alpha = jnp.exp(m_old - m_new)
        p = jnp.exp(s - m_new)
        l_ref[...] = alpha * l_ref[...] + jnp.sum(p, axis=-1, keepdims=True)
        acc_ref[...] = alpha * acc_ref[...] + _dot(p.astype(BF16), k[:, :KV_LORA])
        m_ref[...] = m_new
        return carry

    lax.fori_loop(0, i + 1, body, 0)
    o = acc_ref[...] / l_ref[...]
    yb_ref[...] = _mla_finish(o, wuv_ref, gob_ref, tq)


def _mla_attn(qcat, kcat, w_uvt, g_out_b, batch, seq, tq):
    n = batch * seq
    nq = seq // tq
    rows = MLA_HEADS * tq
    return pl.pallas_call(
        functools.partial(_mla_attn_kernel, tq=tq),
        grid=(batch, nq),
        in_specs=[pl.BlockSpec((MLA_HEADS, tq, QCAT), lambda b, i: (0, b * nq + i, 0)),
                  pl.BlockSpec((seq, QCAT), lambda b, i: (b, 0)),
                  _full(w_uvt.shape), _full((1, MLA_DIM))],
        out_specs=pl.BlockSpec((tq, MLA_DIM), lambda b, i: (b * nq + i, 0)),
        out_shape=jax.ShapeDtypeStruct((n, MLA_DIM), BF16),
        scratch_shapes=[pltpu.VMEM((rows, 1), F32), pltpu.VMEM((rows, 1), F32),
                        pltpu.VMEM((rows, KV_LORA), F32)],
        compiler_params=_cparams(2), name="mla_attn",
    )(qcat, kcat, w_uvt, g_out_b)


def _mla_dec_kernel(q_ref, cc_ref, ck_ref, kn_ref, wuv_ref, gob_ref, yb_ref, *, t):
    rows = MLA_HEADS * t
    q = q_ref[...].reshape(rows, QCAT)
    cc = cc_ref[0].astype(BF16)
    ck = ck_ref[0].astype(BF16)
    kn = kn_ref[...]
    s_c = (_dot_nt(q[:, :KV_LORA], cc) + _dot_nt(q[:, KV_LORA:], ck)) * MLA_SCALE
    s_n = _dot_nt(q, kn) * MLA_SCALE
    m = jnp.maximum(jnp.max(s_c, axis=-1, keepdims=True), jnp.max(s_n, axis=-1, keepdims=True))
    p_c = jnp.exp(s_c - m)
    p_n = jnp.exp(s_n - m)
    l = jnp.sum(p_c, axis=-1, keepdims=True) + jnp.sum(p_n, axis=-1, keepdims=True)
    o = (_dot(p_c.astype(BF16), cc) + _dot(p_n.astype(BF16), kn[:, :KV_LORA])) / l
    yb_ref[...] = _mla_finish(o, wuv_ref, gob_ref, t)


def _mla_dec(qcat, cache_ckv, cache_krope, kcat, w_uvt, g_out_b, batch, t):
    past = cache_ckv.shape[1]
    return pl.pallas_call(
        functools.partial(_mla_dec_kernel, t=t),
        grid=(batch,),
        in_specs=[pl.BlockSpec((MLA_HEADS, t, QCAT), lambda b: (0, b, 0)),
                  pl.BlockSpec((1, past, KV_LORA), lambda b: (b, 0, 0)),
                  pl.BlockSpec((1, past, QK_ROPE), lambda b: (b, 0, 0)),
                  pl.BlockSpec((t, QCAT), lambda b: (b, 0)),
                  _full(w_uvt.shape), _full((1, MLA_DIM))],
        out_specs=pl.BlockSpec((t, MLA_DIM), lambda b: (b, 0)),
        out_shape=jax.ShapeDtypeStruct((batch * t, MLA_DIM), BF16),
        compiler_params=_cparams(1), name="mla_dec",
    )(qcat, cache_ckv, cache_krope, kcat, w_uvt, g_out_b)


def _memkv_kernel(mem_ref, g_ref, wk_ref, wv_ref, mk_ref, mv_ref):
    mn = _rms(mem_ref[...], g_ref[...]).astype(BF16)
    mk_ref[...] = _dot(mn, wk_ref[...])
    mv_ref[...] = _dot(mn, wv_ref[...])


def _memkv(mem, g_mem, w_mk, w_mv):
    n = mem.shape[0]
    tm = MEM_TOKENS
    return pl.pallas_call(
        _memkv_kernel,
        grid=(n // tm,),
        in_specs=[pl.BlockSpec((tm, D_MODEL), lambda i: (i, 0)), _full((1, D_MODEL)),
                  _full(w_mk.shape), _full(w_mv.shape)],
        out_specs=[pl.BlockSpec((tm, MEM_DIM), lambda i: (i, 0))] * 2,
        out_shape=[jax.ShapeDtypeStruct((n, MEM_DIM), F32)] * 2,
        compiler_params=_cparams(1), name="mem_kv",
    )(mem, g_mem, w_mk, w_mv)


def _memattn_kernel(qm_ref, mk_ref, mv_ref, gom_ref, ym_ref):
    qm = qm_ref[...]
    outs = []
    for h in range(MEM_HEADS):
        sl = slice(h * MEM_HEAD_DIM, (h + 1) * MEM_HEAD_DIM)
        s = _dot_nt(qm[:, sl].astype(BF16), mk_ref[:, sl].astype(BF16)) * MEM_SCALE
        e = jnp.exp(s - jnp.max(s, axis=-1, keepdims=True))
        p = e / jnp.sum(e, axis=-1, keepdims=True)
        outs.append(_dot(p.astype(BF16), mv_ref[:, sl].astype(BF16)))
    ym_ref[...] = _rms(jnp.concatenate(outs, axis=-1), gom_ref[...]).astype(BF16)


def _memattn(q_m, mk, mv, g_out_m, tm, tiles_per_batch):
    n = q_m.shape[0]
    kv = pl.BlockSpec((MEM_TOKENS, MEM_DIM), lambda i: (i // tiles_per_batch, 0))
    return pl.pallas_call(
        _memattn_kernel,
        grid=(n // tm,),
        in_specs=[pl.BlockSpec((tm, MEM_DIM), lambda i: (i, 0)), kv, kv, _full((1, MEM_DIM))],
        out_specs=pl.BlockSpec((tm, MEM_DIM), lambda i: (i, 0)),
        out_shape=jax.ShapeDtypeStruct((n, MEM_DIM), BF16),
        compiler_params=_cparams(1), name="mem_attn",
    )(q_m, mk, mv, g_out_m)


def _merge_kernel(x_ref, ya_ref, yb_ref, ym_ref, w_ref, gf_ref, x1_ref, xf_ref):
    a0, a1 = GMLP_DIM, GMLP_DIM + MLA_DIM
    y = (_dot(ya_ref[...], w_ref[:a0, :]) + _dot(yb_ref[...], w_ref[a0:a1, :])
         + _dot(ym_ref[...], w_ref[a1:, :]))
    x1 = x_ref[...] + y
    x1_ref[...] = x1
    xf_ref[...] = _rms(x1, gf_ref[...])


def _merge(x, ya, yb, ym, w_out, g_ffn, tm):
    n = x.shape[0]
    row = lambda w: pl.BlockSpec((tm, w), lambda i: (i, 0))
    return pl.pallas_call(
        _merge_kernel,
        grid=(n // tm,),
        in_specs=[row(D_MODEL), row(GMLP_DIM), row(MLA_DIM), row(MEM_DIM), _full(w_out.shape),
                  _full((1, D_MODEL))],
        out_specs=[row(D_MODEL), row(D_MODEL)],
        out_shape=[jax.ShapeDtypeStruct((n, D_MODEL), F32)] * 2,
        compiler_params=_cparams(1), name="merge",
    )(x, ya, yb, ym, w_out, g_ffn)


def _top16(s, payload):
    n = s.shape[0]
    iota = lax.broadcasted_iota(I32, s.shape, 0)
    vals, picks = [], []
    for _ in range(PEER_TOPK):
        m = jnp.max(s, axis=0, keepdims=True)
        idx = jnp.min(jnp.where(s == m, iota, n), axis=0, keepdims=True)
        sel = iota == idx
        vals.append(m)
        picks.append(idx if payload is None else jnp.max(jnp.where(sel, payload, -1), axis=0, keepdims=True))
        s = jnp.where(sel, -jnp.inf, s)
    return jnp.concatenate(vals, axis=0), jnp.concatenate(picks, axis=0)


def _peer_topk_kernel(xf_ref, wpq_ref, sk1_ref, sk2_ref, eid_ref, gate_ref):
    q = _dot(xf_ref[...].astype(BF16), wpq_ref[...])
    for h in range(PEER_HEADS):
        qa = q[:, h * PEER_QDIM:h * PEER_QDIM + PEER_HALF].astype(BF16)
        qb = q[:, h * PEER_QDIM + PEER_HALF:(h + 1) * PEER_QDIM].astype(BF16)
        v1, i1 = _top16(_dot_nt(sk1_ref[...], qa), None)
        v2, i2 = _top16(_dot_nt(sk2_ref[...], qb), None)
        cand = jnp.concatenate([v1[a:a + 1] + v2 for a in range(PEER_TOPK)], axis=0)
        cand_e = jnp.concatenate([i1[a:a + 1] * N_KEYS + i2 for a in range(PEER_TOPK)], axis=0)
        vals, eid = _top16(cand, cand_e)
        e = jnp.exp(vals - vals[0:1])
        rows = slice(h * PEER_TOPK, (h + 1) * PEER_TOPK)
        gate_ref[rows, :] = e / jnp.sum(e, axis=0, keepdims=True)
        eid_ref[rows, :] = eid


def _peer_topk(xf, w_pq, sk1, sk2, tm):
    n = xf.shape[0]
    out = pl.BlockSpec((PEER_PAIRS, tm), lambda i: (0, i))
    return pl.pallas_call(
        _peer_topk_kernel,
        grid=(n // tm,),
        in_specs=[pl.BlockSpec((tm, D_MODEL), lambda i: (i, 0)), _full(w_pq.shape),
                  _full(sk1.shape), _full(sk2.shape)],
        out_specs=[out, out],
        out_shape=[jax.ShapeDtypeStruct((PEER_PAIRS, n), I32), jax.ShapeDtypeStruct((PEER_PAIRS, n), F32)],
        compiler_params=_cparams(1), name="peer_topk",
    )(xf, w_pq, sk1, sk2)


def _bf16_bits(x):
    return pltpu.bitcast(x.astype(BF16).astype(F32), U32)


def _peer_pack_kernel(pu_ref, pv_ref, tab_ref):
    for src, base in ((pu_ref, 0), (pv_ref, ROW_CHUNKS // 2)):
        bits = _bf16_bits(src[...])
        words = (bits[:, HALF_D:] & jnp.uint32(0xFFFF0000)) | (bits[:, :HALF_D] >> 16)
        for c in range(ROW_CHUNKS // 2):
            tab_ref[:, base + c, :] = words[:, c * LANES:(c + 1) * LANES]


def _peer_pack(peer_u, peer_v):
    ne = peer_u.shape[0]
    tr = 256
    return pl.pallas_call(
        _peer_pack_kernel,
        grid=(ne // tr,),
        in_specs=[pl.BlockSpec((tr, D_MODEL), lambda i: (i, 0))] * 2,
        out_specs=pl.BlockSpec((tr, ROW_CHUNKS, LANES), lambda i: (i, 0, 0)),
        out_shape=jax.ShapeDtypeStruct((ne, ROW_CHUNKS, LANES), U32),
        compiler_params=_cparams(1), name="peer_pack",
    )(peer_u, peer_v)


def _unpack(words):
    lo = pltpu.bitcast(words << 16, F32)
    hi = pltpu.bitcast(words & jnp.uint32(0xFFFF0000), F32)
    return lo, hi


def _peer_mix_kernel(eid_ref, gate_ref, xf_ref, x1_ref, gfin_ref, tab_ref, y_ref, buf_ref, sem, *, tt):
    half_chunks = ROW_CHUNKS // 2

    def issue(t, slot):
        for p in range(PEER_PAIRS):
            pltpu.make_async_copy(tab_ref.at[eid_ref[t, p]], buf_ref.at[slot, p], sem.at[slot]).start(priority=p % 2)

    def wait(slot):
        pltpu.make_async_copy(tab_ref.at[pl.ds(0, PEER_PAIRS)], buf_ref.at[slot], sem.at[slot]).wait()

    lane = lax.broadcasted_iota(I32, (PEER_PAIRS, tt), 1)
    issue(0, 0)

    def body(t, carry):
        slot = t % 2

        @pl.when(t + 1 < tt)
        def _():
            issue(t + 1, 1 - slot)

        wait(slot)
        xrow = xf_ref[t]
        acc = jnp.zeros((PEER_PAIRS, LANES), F32)
        for c in range(half_chunks):
            lo, hi = _unpack(buf_ref[slot, :, c, :])
            acc = acc + lo * xrow[:, c * LANES:(c + 1) * LANES]
            acc = acc + hi * xrow[:, HALF_D + c * LANES:HALF_D + (c + 1) * LANES]
        h = jnp.sum(acc, axis=-1, keepdims=True)
        g = jnp.sum(jnp.where(lane == t, gate_ref[0], 0.0), axis=-1, keepdims=True)
        w = g * _gelu(h)
        los, his = [], []
        for c in range(half_chunks):
            lo, hi = _unpack(buf_ref[slot, :, half_chunks + c, :])
            los.append(jnp.sum(lo * w, axis=0, keepdims=True))
            his.append(jnp.sum(hi * w, axis=0, keepdims=True))
        y_ref[t] = _rms(x1_ref[t] + jnp.concatenate(los + his, axis=-1), gfin_ref[...])
        return carry

    lax.fori_loop(0, tt, body, 0)


def _peer_mix(eid_t, gate_t, xf, x1, g_final, table, tt):
    n = xf.shape[0]
    eid = eid_t.T
    gate = gate_t.reshape(PEER_PAIRS, n // tt, tt).transpose(1, 0, 2)
    row = pl.BlockSpec((tt, 1, D_MODEL), lambda i: (i, 0, 0))
    y = pl.pallas_call(
        functools.partial(_peer_mix_kernel, tt=tt),
        grid=(n // tt,),
        in_specs=[pl.BlockSpec((tt, PEER_PAIRS), lambda i: (i, 0), memory_space=pltpu.SMEM),
                  pl.BlockSpec((1, PEER_PAIRS, tt), lambda i: (i, 0, 0)),
                  row, row, _full((1, D_MODEL)),
                  pl.BlockSpec(memory_space=pl.ANY)],
        out_specs=row,
        out_shape=jax.ShapeDtypeStruct((n, 1, D_MODEL), F32),
        scratch_shapes=[pltpu.VMEM((2, PEER_PAIRS, ROW_CHUNKS, LANES), U32),
                        pltpu.SemaphoreType.DMA((2,))],
        compiler_params=_cparams(1), name="peer_mix",
    )(eid, gate, xf.reshape(n, 1, D_MODEL), x1.reshape(n, 1, D_MODEL), g_final, table)
    return y.reshape(n, D_MODEL)


def _rope_tables(pos):
    half = QK_ROPE // 2
    inv = ROPE_THETA ** (-jnp.arange(half, dtype=F32) / half)
    ang = pos.astype(F32)[:, None] * inv[None, :]
    cos, sin = jnp.cos(ang), jnp.sin(ang)
    return jnp.concatenate([cos, cos], -1), jnp.concatenate([sin, sin], -1)


def _rot_cols(w):
    half = w.shape[-1] // 2
    return jnp.concatenate([-w[..., half:], w[..., :half]], axis=-1)


def _group(x, tokens_per_seq, pos, cache, mem_kv, wts, g_final):
    n = x.shape[0]
    nseq = n // tokens_per_seq
    tm = min(256, n)
    cos2, sin2 = _rope_tables(pos)
    reps = max(tm // tokens_per_seq, 1)
    cos2, sin2 = jnp.tile(cos2, (reps, 1)), jnp.tile(sin2, (reps, 1))
    cos8, sin8 = jnp.tile(cos2, (1, MLA_HEADS)), jnp.tile(sin2, (1, MLA_HEADS))

    uv, c_q, ckv, krope, kcat, q_m = _proj(x, wts["g_attn"], wts["w_ext"], wts["g_kv"], cos2, sin2, tm)
    rows = min(GMLP_CHUNK, tokens_per_seq)
    ya, v = _gmlp(uv, wts["g_v"], wts["b_v"], wts["w_s"], wts["b_st"], wts["g_out_a"], rows)
    qcat = _mlaq(c_q, wts["g_q"], wts["w_uqp"], wts["w_ukt"], cos8, sin8, tm)
    if cache is None:
        yb = _mla_attn(qcat, kcat, wts["w_uvt"], wts["g_out_b"], nseq, tokens_per_seq, min(256, tokens_per_seq))
    else:
        yb = _mla_dec(qcat, cache[0], cache[1], kcat, wts["w_uvt"], wts["g_out_b"], nseq, tokens_per_seq)
    mk, mv = mem_kv
    tma = min(tm, tokens_per_seq)
    ym = _memattn(q_m, mk, mv, wts["g_out_m"], tma, tokens_per_seq // tma)
    x1, xf = _merge(x, ya, yb, ym, wts["w_out"], wts["g_ffn"], tm)
    eid_t, gate_t = _peer_topk(xf, wts["w_pq"], wts["sk1"], wts["sk2"], tm)
    y = _peer_mix(eid_t, gate_t, xf, x1, g_final, wts["table"], min(64, n))
    return y, ckv, krope, v


def kernel(x_prompt, x_sample, cache_mla_ckv, cache_mla_krope, cache_mem_k, cache_mem_v, mem_prompt, g_attn, w_in, g_v, b_v, w_s, b_s, g_q, w_uq, w_uk, w_uv, g_kv, g_mem, w_mk, w_mv, g_out_a, g_out_b, g_out_m, w_out, g_ffn, w_pq, sub_keys1, sub_keys2, peer_u, peer_v, g_final):
    assert w_in.shape[0] == 1, "the final norm is fused after the single layer"
    l = 0
    bp, sp, _ = x_prompt.shape
    bs, ts, _ = x_sample.shape
    past = cache_mla_ckv.shape[2]
    gfin = g_final.reshape(1, D_MODEL)
    wi = w_in[l]
    w_ext = jnp.concatenate([wi[:, :_C_KROT], _rot_cols(wi[:, _C_KR:_C_KROT]), wi[:, _C_KROT:]], axis=1)
    wq = w_uq[l].reshape(Q_LORA, MLA_HEADS, QK_NOPE + QK_ROPE)
    wq_rope = wq[:, :, QK_NOPE:]
    w_uqp = jnp.concatenate([wq[:, :, :QK_NOPE].reshape(Q_LORA, -1), wq_rope.reshape(Q_LORA, -1),
                             _rot_cols(wq_rope).reshape(Q_LORA, -1)], axis=1)
    wts = {
        "g_attn": g_attn[l].reshape(1, -1), "w_ext": w_ext.astype(BF16), "g_kv": g_kv[l].reshape(1, -1),
        "g_v": g_v[l].reshape(1, -1), "b_v": b_v[l].reshape(1, -1), "w_s": w_s[l], "b_st": b_s[l].T,
        "g_out_a": g_out_a[l].reshape(1, -1), "g_q": g_q[l].reshape(1, -1), "w_uqp": w_uqp.astype(BF16),
        "w_ukt": jnp.transpose(w_uk[l], (1, 2, 0)).astype(BF16),
        "w_uvt": jnp.transpose(w_uv[l], (1, 0, 2)).astype(BF16),
        "g_out_b": g_out_b[l].reshape(1, -1), "g_out_m": g_out_m[l].reshape(1, -1),
        "w_out": w_out[l].astype(BF16), "g_ffn": g_ffn[l].reshape(1, -1), "w_pq": w_pq[l].astype(BF16),
        "sk1": sub_keys1[l].astype(BF16), "sk2": sub_keys2[l].astype(BF16),
        "table": _peer_pack(peer_u[l], peer_v[l]),
    }
    mk, mv = _memkv(mem_prompt.reshape(bp * MEM_TOKENS, D_MODEL), g_mem[l].reshape(1, -1),
                    w_mk[l].astype(BF16), w_mv[l].astype(BF16))
    yp, ckv_p, kr_p, _ = _group(x_prompt.reshape(bp * sp, D_MODEL), sp, jnp.arange(sp), None, (mk, mv), wts, gfin)
    mem_s = (cache_mem_k[l].reshape(bs * MEM_TOKENS, MEM_DIM), cache_mem_v[l].reshape(bs * MEM_TOKENS, MEM_DIM))
    ys, ckv_s, kr_s, gv_s = _group(x_sample.reshape(bs * ts, D_MODEL), ts, past + jnp.arange(ts),
                                   (cache_mla_ckv[l], cache_mla_krope[l]), mem_s, wts, gfin)
    return (yp.reshape(bp, sp, D_MODEL), ys.reshape(bs, ts, D_MODEL),
            ckv_p.reshape(1, bp, sp, KV_LORA), kr_p.reshape(1, bp, sp, QK_ROPE),
            mk.reshape(1, bp, MEM_TOKENS, MEM_HEADS, MEM_HEAD_DIM),
            mv.reshape(1, bp, MEM_TOKENS, MEM_HEADS, MEM_HEAD_DIM),
            ckv_s.reshape(1, bs, ts, KV_LORA), kr_s.reshape(1, bs, ts, QK_ROPE),
            gv_s.reshape(1, bs, ts, GMLP_DIM))
```

```python
import functools
import math

import jax
import jax.numpy as jnp
from jax import lax
from jax.experimental import pallas as pl
from jax.experimental.pallas import tpu as pltpu

F32 = jnp.float32
BF16 = jnp.bfloat16
I32 = jnp.int32
WORD = jnp.int32

D_MODEL = 2048
CHUNK = 64
CHUNK_SHIFT = 6
EPS = 1e-6
GMLP_CHUNK = 128
GMLP_DIM = 512
GMLP_GROUPS = 4
GMLP_GROUP_DIM = 128
V_HEAD = 128
QK_NOPE = 128
QK_ROPE = 64
MLA_HEADS = 8
MLA_DIM = 1024
Q_LORA = 512
KV_LORA = 256
ROPE_THETA = 10000.0
MLA_SCALE = (QK_NOPE + QK_ROPE) ** -0.5
QCAT = KV_LORA + QK_ROPE
MEM_TOKENS = 256
MEM_HEADS = 4
MEM_DIM = 512
MEM_HEAD_DIM = 128
MEM_SCALE = MEM_HEAD_DIM ** -0.5
PEER_HEADS = 8
N_KEYS = 128
PEER_QDIM = 256
PEER_HALF = 128
PEER_TOPK = 16
PEER_PAIRS = PEER_HEADS * PEER_TOPK
HALF_D = D_MODEL // 2
LANES = 128
ROW_CHUNKS = 2 * HALF_D // LANES
ROW_PITCH = 24
MIX_SLOTS = 4

_C_UV, _C_Q, _C_KV, _C_KR, _C_KROT, _C_M, _C_END = 0, 1024, 1536, 1792, 1856, 1920, 2432

VMEM_LIMIT = 48 * 1024 * 1024


def _cparams(n_grid):
    return pltpu.CompilerParams(dimension_semantics=("arbitrary",) * n_grid,
                                vmem_limit_bytes=VMEM_LIMIT)


def _rms(x, g):
    return x * lax.rsqrt(jnp.mean(x * x, axis=-1, keepdims=True) + EPS) * g


def _gelu(x):
    return x * (0.5 * (1.0 + jnp.tanh(math.sqrt(2.0 / math.pi) * (x + 0.044715 * (x * x * x)))))


def _dot(a, b):
    return jnp.dot(a, b, preferred_element_type=F32)


def _dot_nt(a, b):
    return lax.dot_general(a, b, (((1,), (1,)), ((), ())), preferred_element_type=F32)


def _full(shape):
    n = len(shape)
    return pl.BlockSpec(shape, lambda *_: (0,) * n)


def _proj_kernel(x_ref, g_ref, w_ref, gkv_ref, cos_ref, sin_ref,
                 uv_ref, cq_ref, ckv_ref, kr_ref, kcat_ref, qm_ref):
    xn = _rms(x_ref[...], g_ref[...])
    proj = _dot(xn.astype(BF16), w_ref[...])
    uv_ref[...] = proj[:, _C_UV:_C_Q]
    cq_ref[...] = proj[:, _C_Q:_C_KV]
    ckv = _rms(proj[:, _C_KV:_C_KR], gkv_ref[...])
    ckv_ref[...] = ckv
    krope = proj[:, _C_KR:_C_KROT] * cos_ref[...] + proj[:, _C_KROT:_C_M] * sin_ref[...]
    kr_ref[...] = krope
    kcat_ref[:, :KV_LORA] = ckv.astype(BF16)
    kcat_ref[:, KV_LORA:] = krope.astype(BF16)
    qm_ref[...] = proj[:, _C_M:_C_END]


def _proj(x, g_attn, w_ext, g_kv, cos2, sin2, tm):
    n = x.shape[0]
    nper = cos2.shape[0] // tm
    row = lambda w: pl.BlockSpec((tm, w), lambda i: (i, 0))
    tab = pl.BlockSpec((tm, QK_ROPE), lambda i: (i % nper, 0))
    return pl.pallas_call(
        _proj_kernel,
        grid=(n // tm,),
        in_specs=[row(D_MODEL), _full((1, D_MODEL)), _full(w_ext.shape), _full((1, KV_LORA)), tab, tab],
        out_specs=[row(2 * GMLP_DIM), row(Q_LORA), row(KV_LORA), row(QK_ROPE), row(QCAT), row(MEM_DIM)],
        out_shape=[jax.ShapeDtypeStruct((n, 2 * GMLP_DIM), F32), jax.ShapeDtypeStruct((n, Q_LORA), F32),
                   jax.ShapeDtypeStruct((n, KV_LORA), F32), jax.ShapeDtypeStruct((n, QK_ROPE), F32),
                   jax.ShapeDtypeStruct((n, QCAT), BF16), jax.ShapeDtypeStruct((n, MEM_DIM), F32)],
        compiler_params=_cparams(1), name="proj",
    )(x, g_attn, w_ext, g_kv, cos2, sin2)


def _gmlp_kernel(uv_ref, gv_ref, bv_ref, ws_ref, bst_ref, goa_ref, ya_ref, v_ref, *, rows):
    uv = _gelu(uv_ref[...])
    u = uv[:, :GMLP_DIM]
    vr = uv[:, GMLP_DIM:]
    mu = jnp.mean(vr, axis=-1, keepdims=True)
    var = jnp.mean(jnp.square(vr - mu), axis=-1, keepdims=True)
    v = (vr - mu) * lax.rsqrt(var + EPS) * gv_ref[...] + bv_ref[...]
    v_ref[...] = v
    r = lax.broadcasted_iota(I32, (rows, rows), 0)
    c = lax.broadcasted_iota(I32, (rows, rows), 1)
    zs = []
    for g in range(GMLP_GROUPS):
        w = jnp.where(r >= c, ws_ref[g, :rows, :rows], 0.0).astype(BF16)
        vg = v[:, g * GMLP_GROUP_DIM:(g + 1) * GMLP_GROUP_DIM].astype(BF16)
        zs.append(_dot(w, vg) + bst_ref[:rows, g:g + 1])
    y = u * jnp.concatenate(zs, axis=-1)
    ya_ref[...] = _rms(y, goa_ref[...]).astype(BF16)


def _gmlp(uv, g_v, b_v, w_s, b_st, g_out_a, rows):
    n = uv.shape[0]
    row = lambda w: pl.BlockSpec((rows, w), lambda i: (i, 0))
    return pl.pallas_call(
        functools.partial(_gmlp_kernel, rows=rows),
        grid=(n // rows,),
        in_specs=[row(2 * GMLP_DIM), _full((1, GMLP_DIM)), _full((1, GMLP_DIM)), _full(w_s.shape),
                  _full(b_st.shape), _full((1, GMLP_DIM))],
        out_specs=[row(GMLP_DIM), row(GMLP_DIM)],
        out_shape=[jax.ShapeDtypeStruct((n, GMLP_DIM), BF16), jax.ShapeDtypeStruct((n, GMLP_DIM), F32)],
        compiler_params=_cparams(1), name="gmlp",
    )(uv, g_v, b_v, w_s, b_st, g_out_a)


def _mlaq_kernel(cq_ref, gq_ref, wq_ref, wuk_ref, cos_ref, sin_ref, q_ref):
    cq = _rms(cq_ref[...], gq_ref[...])
    q = _dot(cq.astype(BF16), wq_ref[...])
    nr = MLA_HEADS * QK_NOPE
    rw = MLA_HEADS * QK_ROPE
    qrope = q[:, nr:nr + rw] * cos_ref[...] + q[:, nr + rw:] * sin_ref[...]
    for h in range(MLA_HEADS):
        qn = q[:, h * QK_NOPE:(h + 1) * QK_NOPE].astype(BF16)
        q_ref[h, :, :KV_LORA] = _dot(qn, wuk_ref[h]).astype(BF16)
        q_ref[h, :, KV_LORA:] = qrope[:, h * QK_ROPE:(h + 1) * QK_ROPE].astype(BF16)


def _mlaq(c_q, g_q, w_uqp, w_ukt, cos8, sin8, tm):
    n = c_q.shape[0]
    nper = cos8.shape[0] // tm
    tab = pl.BlockSpec((tm, MLA_HEADS * QK_ROPE), lambda i: (i % nper, 0))
    return pl.pallas_call(
        _mlaq_kernel,
        grid=(n // tm,),
        in_specs=[pl.BlockSpec((tm, Q_LORA), lambda i: (i, 0)), _full((1, Q_LORA)), _full(w_uqp.shape),
                  _full(w_ukt.shape), tab, tab],
        out_specs=pl.BlockSpec((MLA_HEADS, tm, QCAT), lambda i: (0, i, 0)),
        out_shape=jax.ShapeDtypeStruct((MLA_HEADS, n, QCAT), BF16),
        compiler_params=_cparams(1), name="mla_q",
    )(c_q, g_q, w_uqp, w_ukt, cos8, sin8)


def _mla_finish(o, wuv_ref, gob_ref, tq):
    ys = [_dot(o[h * tq:(h + 1) * tq].astype(BF16), wuv_ref[h]) for h in range(MLA_HEADS)]
    return _rms(jnp.concatenate(ys, axis=-1), gob_ref[...]).astype(BF16)


def _mla_attn_kernel(q_ref, k_ref, wuv_ref, gob_ref, yb_ref, m_ref, l_ref, acc_ref, *, tq):
    i = pl.program_id(1)
    rows = MLA_HEADS * tq
    q = q_ref[...].reshape(rows, QCAT)
    m_ref[...] = jnp.full((rows, 1), -jnp.inf, F32)
    l_ref[...] = jnp.zeros((rows, 1), F32)
    acc_ref[...] = jnp.zeros((rows, KV_LORA), F32)
    qpos = i * tq + lax.broadcasted_iota(I32, (MLA_HEADS, tq, tq), 1).reshape(rows, tq)
    kofs = lax.broadcasted_iota(I32, (rows, tq), 1)

    def body(j, carry):
        k = k_ref[pl.ds(pl.multiple_of(j * tq, tq), tq), :]
        s = _dot_nt(q, k) * MLA_SCALE
        s = jnp.where(((j * tq + kofs) >> CHUNK_SHIFT) <= (qpos >> CHUNK_SHIFT), s, -jnp.inf)
        m_old = m_ref[...]
        m_new = jnp.maximum(m_old, jnp.max(s, axis=-1, keepdims=True))
        alpha = jnp.exp(m_old - m_new)
        p = jnp.exp(s - m_new)
        l_ref[...] = alpha * l_ref[...] + jnp.sum(p, axis=-1, keepdims=True)
        acc_ref[...] = alpha * acc_ref[...] + _dot(p.astype(BF16), k[:, :KV_LORA])
        m_ref[...] = m_new
        return carry

    lax.fori_loop(0, i + 1, body, 0)
    o = acc_ref[...] / l_ref[...]
    yb_ref[...] = _mla_finish(o, wuv_ref, gob_ref, tq)


def _mla_attn(qcat, kcat, w_uvt, g_out_b, batch, seq, tq):
    n = batch * seq
    nq = seq // tq
    rows = MLA_HEADS * tq
    return pl.pallas_call(
        functools.partial(_mla_attn_kernel, tq=tq),
        grid=(batch, nq),
        in_specs=[pl.BlockSpec((MLA_HEADS, tq, QCAT), lambda b, i: (0, b * nq + i, 0)),
                  pl.BlockSpec((seq, QCAT), lambda b, i: (b, 0)),
                  _full(w_uvt.shape), _full((1, MLA_DIM))],
        out_specs=pl.BlockSpec((tq, MLA_DIM), lambda b, i: (b * nq + i, 0)),
        out_shape=jax.ShapeDtypeStruct((n, MLA_DIM), BF16),
        scratch_shapes=[pltpu.VMEM((rows, 1), F32), pltpu.VMEM((rows, 1), F32),
                        pltpu.VMEM((rows, KV_LORA), F32)],
        compiler_params=_cparams(2), name="mla_attn",
    )(qcat, kcat, w_uvt, g_out_b)


def _mla_dec_kernel(q_ref, cc_ref, ck_ref, kn_ref, wuv_ref, gob_ref, yb_ref, *, t):
    rows = MLA_HEADS * t
    q = q_ref[...].reshape(rows, QCAT)
    cc = cc_ref[0].astype(BF16)
    ck = ck_ref[0].astype(BF16)
    kn = kn_ref[...]
    s_c = (_dot_nt(q[:, :KV_LORA], cc) + _dot_nt(q[:, KV_LORA:], ck)) * MLA_SCALE
    s_n = _dot_nt(q, kn) * MLA_SCALE
    m = jnp.maximum(jnp.max(s_c, axis=-1, keepdims=True), jnp.max(s_n, axis=-1, keepdims=True))
    p_c = jnp.exp(s_c - m)
    p_n = jnp.exp(s_n - m)
    l = jnp.sum(p_c, axis=-1, keepdims=True) + jnp.sum(p_n, axis=-1, keepdims=True)
    o = (_dot(p_c.astype(BF16), cc) + _dot(p_n.astype(BF16), kn[:, :KV_LORA])) / l
    yb_ref[...] = _mla_finish(o, wuv_ref, gob_ref, t)


def _mla_dec(qcat, cache_ckv, cache_krope, kcat, w_uvt, g_out_b, batch, t):
    past = cache_ckv.shape[1]
    return pl.pallas_call(
        functools.partial(_mla_dec_kernel, t=t),
        grid=(batch,),
        in_specs=[pl.BlockSpec((MLA_HEADS, t, QCAT), lambda b: (0, b, 0)),
                  pl.BlockSpec((1, past, KV_LORA), lambda b: (b, 0, 0)),
                  pl.BlockSpec((1, past, QK_ROPE), lambda b: (b, 0, 0)),
                  pl.BlockSpec((t, QCAT), lambda b: (b, 0)),
                  _full(w_uvt.shape), _full((1, MLA_DIM))],
        out_specs=pl.BlockSpec((t, MLA_DIM), lambda b: (b, 0)),
        out_shape=jax.ShapeDtypeStruct((batch * t, MLA_DIM), BF16),
        compiler_params=_cparams(1), name="mla_dec",
    )(qcat, cache_ckv, cache_krope, kcat, w_uvt, g_out_b)


def _memkv_kernel(mem_ref, g_ref, wk_ref, wv_ref, mk_ref, mv_ref):
    mn = _rms(mem_ref[...], g_ref[...]).astype(BF16)
    mk_ref[...] = _dot(mn, wk_ref[...])
    mv_ref[...] = _dot(mn, wv_ref[...])


def _memkv(mem, g_mem, w_mk, w_mv):
    n = mem.shape[0]
    tm = MEM_TOKENS
    return pl.pallas_call(
        _memkv_kernel,
        grid=(n // tm,),
        in_specs=[pl.BlockSpec((tm, D_MODEL), lambda i: (i, 0)), _full((1, D_MODEL)),
                  _full(w_mk.shape), _full(w_mv.shape)],
        out_specs=[pl.BlockSpec((tm, MEM_DIM), lambda i: (i, 0))] * 2,
        out_shape=[jax.ShapeDtypeStruct((n, MEM_DIM), F32)] * 2,
        compiler_params=_cparams(1), name="mem_kv",
    )(mem, g_mem, w_mk, w_mv)


def _memattn_kernel(qm_ref, mk_ref, mv_ref, gom_ref, ym_ref):
    qm = qm_ref[...]
    outs = []
    for h in range(MEM_HEADS):
        sl = slice(h * MEM_HEAD_DIM, (h + 1) * MEM_HEAD_DIM)
        s = _dot_nt(qm[:, sl].astype(BF16), mk_ref[:, sl].astype(BF16)) * MEM_SCALE
        e = jnp.exp(s - jnp.max(s, axis=-1, keepdims=True))
        p = e / jnp.sum(e, axis=-1, keepdims=True)
        outs.append(_dot(p.astype(BF16), mv_ref[:, sl].astype(BF16)))
    ym_ref[...] = _rms(jnp.concatenate(outs, axis=-1), gom_ref[...]).astype(BF16)


def _memattn(q_m, mk, mv, g_out_m, tm, tiles_per_batch):
    n = q_m.shape[0]
    kv = pl.BlockSpec((MEM_TOKENS, MEM_DIM), lambda i: (i // tiles_per_batch, 0))
    return pl.pallas_call(
        _memattn_kernel,
        grid=(n // tm,),
        in_specs=[pl.BlockSpec((tm, MEM_DIM), lambda i: (i, 0)), kv, kv, _full((1, MEM_DIM))],
        out_specs=pl.BlockSpec((tm, MEM_DIM), lambda i: (i, 0)),
        out_shape=jax.ShapeDtypeStruct((n, MEM_DIM), BF16),
        compiler_params=_cparams(1), name="mem_attn",
    )(q_m, mk, mv, g_out_m)


def _merge_kernel(x_ref, ya_ref, yb_ref, ym_ref, w_ref, gf_ref, x1_ref, xf_ref):
    a0, a1 = GMLP_DIM, GMLP_DIM + MLA_DIM
    y = (_dot(ya_ref[...], w_ref[:a0, :]) + _dot(yb_ref[...], w_ref[a0:a1, :])
         + _dot(ym_ref[...], w_ref[a1:, :]))
    x1 = x_ref[...] + y
    x1_ref[...] = x1
    xf_ref[...] = _rms(x1, gf_ref[...])


def _merge(x, ya, yb, ym, w_out, g_ffn, tm):
    n = x.shape[0]
    row = lambda w: pl.BlockSpec((tm, w), lambda i: (i, 0))
    return pl.pallas_call(
        _merge_kernel,
        grid=(n // tm,),
        in_specs=[row(D_MODEL), row(GMLP_DIM), row(MLA_DIM), row(MEM_DIM), _full(w_out.shape),
                  _full((1, D_MODEL))],
        out_specs=[row(D_MODEL), row(D_MODEL)],
        out_shape=[jax.ShapeDtypeStruct((n, D_MODEL), F32)] * 2,
        compiler_params=_cparams(1), name="merge",
    )(x, ya, yb, ym, w_out, g_ffn)


def _top16(s, payload):
    n = s.shape[0]
    iota = lax.broadcasted_iota(I32, s.shape, 0)
    vals, picks = [], []
    for _ in range(PEER_TOPK):
        m = jnp.max(s, axis=0, keepdims=True)
        idx = jnp.min(jnp.where(s == m, iota, n), axis=0, keepdims=True)
        sel = iota == idx
        vals.append(m)
        picks.append(idx if payload is None else jnp.max(jnp.where(sel, payload, -1), axis=0, keepdims=True))
        s = jnp.where(sel, -jnp.inf, s)
    return jnp.concatenate(vals, axis=0), jnp.concatenate(picks, axis=0)


def _peer_topk_kernel(xf_ref, wpq_ref, sk1_ref, sk2_ref, eid_ref, gate_ref):
    q = _dot(xf_ref[...].astype(BF16), wpq_ref[...])
    for h in range(PEER_HEADS):
        qa = q[:, h * PEER_QDIM:h * PEER_QDIM + PEER_HALF].astype(BF16)
        qb = q[:, h * PEER_QDIM + PEER_HALF:(h + 1) * PEER_QDIM].astype(BF16)
        v1, i1 = _top16(_dot_nt(sk1_ref[...], qa), None)
        v2, i2 = _top16(_dot_nt(sk2_ref[...], qb), None)
        cand = jnp.concatenate([v1[a:a + 1] + v2 for a in range(PEER_TOPK)], axis=0)
        cand_e = jnp.concatenate([i1[a:a + 1] * N_KEYS + i2 for a in range(PEER_TOPK)], axis=0)
        vals, eid = _top16(cand, cand_e)
        e = jnp.exp(vals - vals[0:1])
        rows = slice(h * PEER_TOPK, (h + 1) * PEER_TOPK)
        gate_ref[rows, :] = e / jnp.sum(e, axis=0, keepdims=True)
        eid_ref[rows, :] = eid


def _peer_topk(xf, w_pq, sk1, sk2, tm):
    n = xf.shape[0]
    out = pl.BlockSpec((PEER_PAIRS, tm), lambda i: (0, i))
    return pl.pallas_call(
        _peer_topk_kernel,
        grid=(n // tm,),
        in_specs=[pl.BlockSpec((tm, D_MODEL), lambda i: (i, 0)), _full(w_pq.shape),
                  _full(sk1.shape), _full(sk2.shape)],
        out_specs=[out, out],
        out_shape=[jax.ShapeDtypeStruct((PEER_PAIRS, n), I32), jax.ShapeDtypeStruct((PEER_PAIRS, n), F32)],
        compiler_params=_cparams(1), name="peer_topk",
    )(xf, w_pq, sk1, sk2)


def _peer_pack_kernel(pu_ref, pv_ref, tab_ref, *, tr):
    for src, base in ((pu_ref, 0), (pv_ref, ROW_CHUNKS // 2)):
        x = src[...]
        words = pltpu.pack_elementwise([x[:, :HALF_D], x[:, HALF_D:]], packed_dtype=BF16)
        for c in range(ROW_CHUNKS // 2):
            tab_ref[pl.ds(base + c, tr, stride=ROW_CHUNKS), :] = words[:, c * LANES:(c + 1) * LANES]


def _peer_pack(peer_u, peer_v):
    ne = peer_u.shape[0]
    tr = 256
    return pl.pallas_call(
        functools.partial(_peer_pack_kernel, tr=tr),
        grid=(ne // tr,),
        in_specs=[pl.BlockSpec((tr, D_MODEL), lambda i: (i, 0))] * 2,
        out_specs=pl.BlockSpec((tr * ROW_CHUNKS, LANES), lambda i: (i, 0)),
        out_shape=jax.ShapeDtypeStruct((ne * ROW_CHUNKS, LANES), WORD),
        compiler_params=_cparams(1), name="peer_pack",
    )(peer_u, peer_v)


def _unpack(words):
    return tuple(pltpu.unpack_elementwise(words, index=i, packed_dtype=BF16, unpacked_dtype=F32) for i in (0, 1))


def _peer_mix_kernel(eid_ref, gate_ref, xf_ref, x1_ref, gfin_ref, tab_ref, y_ref, *scratch, tt):
    bufs, sems = scratch[:MIX_SLOTS], scratch[MIX_SLOTS]
    half_chunks = ROW_CHUNKS // 2

    def issue(t, k):
        for p in range(PEER_PAIRS):
            row0 = pl.multiple_of(eid_ref[t, p] * ROW_CHUNKS, ROW_CHUNKS)
            pltpu.make_async_copy(tab_ref.at[pl.ds(row0, ROW_CHUNKS)],
                                  bufs[k].at[pl.ds(p * ROW_PITCH, ROW_CHUNKS)],
                                  sems.at[k]).start(priority=p % 2)

    def wait(k):
        n = PEER_PAIRS * ROW_CHUNKS
        pltpu.make_async_copy(tab_ref.at[pl.ds(0, n)], bufs[k].at[pl.ds(0, n)], sems.at[k]).wait()

    def chunk(k, c):
        return _unpack(bufs[k][pl.ds(c, PEER_PAIRS, stride=ROW_PITCH), :])

    lane = lax.broadcasted_iota(I32, (PEER_PAIRS, tt), 1)

    def mix(t, k):
        xrow = xf_ref[t]
        acc = jnp.zeros((PEER_PAIRS, LANES), F32)
        for c in range(half_chunks):
            lo, hi = chunk(k, c)
            acc = acc + lo * xrow[:, c * LANES:(c + 1) * LANES]
            acc = acc + hi * xrow[:, HALF_D + c * LANES:HALF_D + (c + 1) * LANES]
        h = jnp.sum(acc, axis=-1, keepdims=True)
        g = jnp.sum(jnp.where(lane == t, gate_ref[0], 0.0), axis=-1, keepdims=True)
        w = g * _gelu(h)
        los, his = [], []
        for c in range(half_chunks):
            lo, hi = chunk(k, half_chunks + c)
            los.append(jnp.sum(lo * w, axis=0, keepdims=True))
            his.append(jnp.sum(hi * w, axis=0, keepdims=True))
        y_ref[t] = _rms(x1_ref[t] + jnp.concatenate(los + his, axis=-1), gfin_ref[...])

    ahead = MIX_SLOTS - 1
    for k in range(ahead):
        issue(k, k)

    def group(g, carry):
        for k in range(MIX_SLOTS):
            t = g * MIX_SLOTS + k
            wait(k)
            issue(t + ahead, (k + ahead) % MIX_SLOTS)
            mix(t, k)
        return carry

    groups = tt // MIX_SLOTS
    lax.fori_loop(0, groups - 1, group, 0)
    for k in range(MIX_SLOTS):
        t = (groups - 1) * MIX_SLOTS + k
        wait(k)
        if t + ahead < tt:
            issue(t + ahead, (k + ahead) % MIX_SLOTS)
        mix(t, k)


def _peer_mix(eid_t, gate_t, xf, x1, g_final, table, tt):
    n = xf.shape[0]
    eid = eid_t.T
    gate = gate_t.reshape(PEER_PAIRS, n // tt, tt).transpose(1, 0, 2)
    row = pl.BlockSpec((tt, 1, D_MODEL), lambda i: (i, 0, 0))
    y = pl.pallas_call(
        functools.partial(_peer_mix_kernel, tt=tt),
        grid=(n // tt,),
        in_specs=[pl.BlockSpec((tt, PEER_PAIRS), lambda i: (i, 0), memory_space=pltpu.SMEM),
                  pl.BlockSpec((1, PEER_PAIRS, tt), lambda i: (i, 0, 0)),
                  row, row, _full((1, D_MODEL)),
                  pl.BlockSpec(memory_space=pl.ANY)],
        out_specs=row,
        out_shape=jax.ShapeDtypeStruct((n, 1, D_MODEL), F32),
        scratch_shapes=[pltpu.VMEM((PEER_PAIRS * ROW_PITCH, LANES), WORD)] * MIX_SLOTS
        + [pltpu.SemaphoreType.DMA((MIX_SLOTS,))],
        compiler_params=_cparams(1), name="peer_mix",
    )(eid, gate, xf.reshape(n, 1, D_MODEL), x1.reshape(n, 1, D_MODEL), g_final, table)
    return y.reshape(n, D_MODEL)


def _rope_tables(pos):
    half = QK_ROPE // 2
    inv = ROPE_THETA ** (-jnp.arange(half, dtype=F32) / half)
    ang = pos.astype(F32)[:, None] * inv[None, :]
    cos, sin = jnp.cos(ang), jnp.sin(ang)
    return jnp.concatenate([cos, cos], -1), jnp.concatenate([sin, sin], -1)


def _rot_cols(w):
    half = w.shape[-1] // 2
    return jnp.concatenate([-w[..., half:], w[..., :half]], axis=-1)


def _group(x, tokens_per_seq, pos, cache, mem_kv, wts, g_final):
    n = x.shape[0]
    nseq = n // tokens_per_seq
    tm = min(256, n)
    cos2, sin2 = _rope_tables(pos)
    reps = max(tm // tokens_per_seq, 1)
    cos2, sin2 = jnp.tile(cos2, (reps, 1)), jnp.tile(sin2, (reps, 1))
    cos8, sin8 = jnp.tile(cos2, (1, MLA_HEADS)), jnp.tile(sin2, (1, MLA_HEADS))

    uv, c_q, ckv, krope, kcat, q_m = _proj(x, wts["g_attn"], wts["w_ext"], wts["g_kv"], cos2, sin2, tm)
    rows = min(GMLP_CHUNK, tokens_per_seq)
    ya, v = _gmlp(uv, wts["g_v"], wts["b_v"], wts["w_s"], wts["b_st"], wts["g_out_a"], rows)
    qcat = _mlaq(c_q, wts["g_q"], wts["w_uqp"], wts["w_ukt"], cos8, sin8, tm)
    if cache is None:
        yb = _mla_attn(qcat, kcat, wts["w_uvt"], wts["g_out_b"], nseq, tokens_per_seq, min(256, tokens_per_seq))
    else:
        yb = _mla_dec(qcat, cache[0], cache[1], kcat, wts["w_uvt"], wts["g_out_b"], nseq, tokens_per_seq)
    mk, mv = mem_kv
    tma = min(tm, tokens_per_seq)
    ym = _memattn(q_m, mk, mv, wts["g_out_m"], tma, tokens_per_seq // tma)
    x1, xf = _merge(x, ya, yb, ym, wts["w_out"], wts["g_ffn"], tm)
    eid_t, gate_t = _peer_topk(xf, wts["w_pq"], wts["sk1"], wts["sk2"], tm)
    y = _peer_mix(eid_t, gate_t, xf, x1, g_final, wts["table"], min(64, n))
    return y, ckv, krope, v


def kernel(x_prompt, x_sample, cache_mla_ckv, cache_mla_krope, cache_mem_k, cache_mem_v, mem_prompt, g_attn, w_in, g_v, b_v, w_s, b_s, g_q, w_uq, w_uk, w_uv, g_kv, g_mem, w_mk, w_mv, g_out_a, g_out_b, g_out_m, w_out, g_ffn, w_pq, sub_keys1, sub_keys2, peer_u, peer_v, g_final):
    assert w_in.shape[0] == 1, "the final norm is fused after the single layer"
    l = 0
    bp, sp, _ = x_prompt.shape
    bs, ts, _ = x_sample.shape
    past = cache_mla_ckv.shape[2]
    gfin = g_final.reshape(1, D_MODEL)
    wi = w_in[l]
    w_ext = jnp.concatenate([wi[:, :_C_KROT], _rot_cols(wi[:, _C_KR:_C_KROT]), wi[:, _C_KROT:]], axis=1)
    wq = w_uq[l].reshape(Q_LORA, MLA_HEADS, QK_NOPE + QK_ROPE)
    wq_rope = wq[:, :, QK_NOPE:]
    w_uqp = jnp.concatenate([wq[:, :, :QK_NOPE].reshape(Q_LORA, -1), wq_rope.reshape(Q_LORA, -1),
                             _rot_cols(wq_rope).reshape(Q_LORA, -1)], axis=1)
    wts = {
        "g_attn": g_attn[l].reshape(1, -1), "w_ext": w_ext.astype(BF16), "g_kv": g_kv[l].reshape(1, -1),
        "g_v": g_v[l].reshape(1, -1), "b_v": b_v[l].reshape(1, -1), "w_s": w_s[l], "b_st": b_s[l].T,
        "g_out_a": g_out_a[l].reshape(1, -1), "g_q": g_q[l].reshape(1, -1), "w_uqp": w_uqp.astype(BF16),
        "w_ukt": jnp.transpose(w_uk[l], (1, 2, 0)).astype(BF16),
        "w_uvt": jnp.transpose(w_uv[l], (1, 0, 2)).astype(BF16),
        "g_out_b": g_out_b[l].reshape(1, -1), "g_out_m": g_out_m[l].reshape(1, -1),
        "w_out": w_out[l].astype(BF16), "g_ffn": g_ffn[l].reshape(1, -1), "w_pq": w_pq[l].astype(BF16),
        "sk1": sub_keys1[l].astype(BF16), "sk2": sub_keys2[l].astype(BF16),
        "table": _peer_pack(peer_u[l], peer_v[l]),
    }
    mk, mv = _memkv(mem_prompt.reshape(bp * MEM_TOKENS, D_MODEL), g_mem[l].reshape(1, -1),
                    w_mk[l].astype(BF16), w_mv[l].astype(BF16))
    yp, ckv_p, kr_p, _ = _group(x_prompt.reshape(bp * sp, D_MODEL), sp, jnp.arange(sp), None, (mk, mv), wts, gfin)
    mem_s = (cache_mem_k[l].reshape(bs * MEM_TOKENS, MEM_DIM), cache_mem_v[l].reshape(bs * MEM_TOKENS, MEM_DIM))
    ys, ckv_s, kr_s, gv_s = _group(x_sample.reshape(bs * ts, D_MODEL), ts, past + jnp.arange(ts),
                                   (cache_mla_ckv[l], cache_mla_krope[l]), mem_s, wts, gfin)
    return (yp.reshape(bp, sp, D_MODEL), ys.reshape(bs, ts, D_MODEL),
            ckv_p.reshape(1, bp, sp, KV_LORA), kr_p.reshape(1, bp, sp, QK_ROPE),
            mk.reshape(1, bp, MEM_TOKENS, MEM_HEADS, MEM_HEAD_DIM),
            mv.reshape(1, bp, MEM_TOKENS, MEM_HEADS, MEM_HEAD_DIM),
            ckv_s.reshape(1, bs, ts, KV_LORA), kr_s.reshape(1, bs, ts, QK_ROPE),
            gv_s.reshape(1, bs, ts, GMLP_DIM))
```

```python
import functools
import math

import jax
import jax.numpy as jnp
from jax import lax
from jax.experimental import pallas as pl
from jax.experimental.pallas import tpu as pltpu

F32 = jnp.float32
BF16 = jnp.bfloat16
I32 = jnp.int32
WORD = jnp.uint32

D_MODEL = 2048
CHUNK = 64
CHUNK_SHIFT = 6
EPS = 1e-6
GMLP_CHUNK = 128
GMLP_DIM = 512
GMLP_GROUPS = 4
GMLP_GROUP_DIM = 128
V_HEAD = 128
QK_NOPE = 128
QK_ROPE = 64
MLA_HEADS = 8
MLA_DIM = 1024
Q_LORA = 512
KV_LORA = 256
ROPE_THETA = 10000.0
MLA_SCALE = (QK_NOPE + QK_ROPE) ** -0.5
QCAT = KV_LORA + QK_ROPE
MEM_TOKENS = 256
MEM_HEADS = 4
MEM_DIM = 512
MEM_HEAD_DIM = 128
MEM_SCALE = MEM_HEAD_DIM ** -0.5
PEER_HEADS = 8
N_KEYS = 128
PEER_QDIM = 256
PEER_HALF = 128
PEER_TOPK = 16
PEER_PAIRS = PEER_HEADS * PEER_TOPK
PAIR_A_FULL = 4
PAIR_B_COLS = 3
BIG_ORDER = 1e9
HALF_D = D_MODEL // 2
LANES = 128
ROW_CHUNKS = 2 * HALF_D // LANES
ROW_PITCH = 24
MIX_SLOTS = 4
MIX_AHEAD = 2

_C_UV, _C_Q, _C_KV, _C_KR, _C_KROT, _C_M, _C_END = 0, 1024, 1536, 1792, 1856, 1920, 2432

VMEM_LIMIT = 48 * 1024 * 1024


def _cparams(n_grid):
    return pltpu.CompilerParams(dimension_semantics=("arbitrary",) * n_grid,
                                vmem_limit_bytes=VMEM_LIMIT)


def _rms(x, g):
    return x * lax.rsqrt(jnp.mean(x * x, axis=-1, keepdims=True) + EPS) * g


def _gelu(x):
    return x * (0.5 * (1.0 + jnp.tanh(math.sqrt(2.0 / math.pi) * (x + 0.044715 * (x * x * x)))))


def _dot(a, b):
    return jnp.dot(a, b, preferred_element_type=F32)


def _dot_nt(a, b):
    return lax.dot_general(a, b, (((1,), (1,)), ((), ())), preferred_element_type=F32)


def _full(shape):
    n = len(shape)
    return pl.BlockSpec(shape, lambda *_: (0,) * n)


def _proj_kernel(x_ref, g_ref, w_ref, gkv_ref, cos_ref, sin_ref,
                 uv_ref, cq_ref, ckv_ref, kr_ref, kcat_ref, qm_ref):
    xn = _rms(x_ref[...], g_ref[...])
    proj = _dot(xn.astype(BF16), w_ref[...])
    uv_ref[...] = proj[:, _C_UV:_C_Q]
    cq_ref[...] = proj[:, _C_Q:_C_KV]
    ckv = _rms(proj[:, _C_KV:_C_KR], gkv_ref[...])
    ckv_ref[...] = ckv
    krope = proj[:, _C_KR:_C_KROT] * cos_ref[...] + proj[:, _C_KROT:_C_M] * sin_ref[...]
    kr_ref[...] = krope
    kcat_ref[:, :KV_LORA] = ckv.astype(BF16)
    kcat_ref[:, KV_LORA:] = krope.astype(BF16)
    qm_ref[...] = proj[:, _C_M:_C_END]


def _proj(x, g_attn, w_ext, g_kv, cos2, sin2, tm):
    n = x.shape[0]
    nper = cos2.shape[0] // tm
    row = lambda w: pl.BlockSpec((tm, w), lambda i: (i, 0))
    tab = pl.BlockSpec((tm, QK_ROPE), lambda i: (i % nper, 0))
    return pl.pallas_call(
        _proj_kernel,
        grid=(n // tm,),
        in_specs=[row(D_MODEL), _full((1, D_MODEL)), _full(w_ext.shape), _full((1, KV_LORA)), tab, tab],
        out_specs=[row(2 * GMLP_DIM), row(Q_LORA), row(KV_LORA), row(QK_ROPE), row(QCAT), row(MEM_DIM)],
        out_shape=[jax.ShapeDtypeStruct((n, 2 * GMLP_DIM), F32), jax.ShapeDtypeStruct((n, Q_LORA), F32),
                   jax.ShapeDtypeStruct((n, KV_LORA), F32), jax.ShapeDtypeStruct((n, QK_ROPE), F32),
                   jax.ShapeDtypeStruct((n, QCAT), BF16), jax.ShapeDtypeStruct((n, MEM_DIM), F32)],
        compiler_params=_cparams(1), name="proj",
    )(x, g_attn, w_ext, g_kv, cos2, sin2)


def _gmlp_kernel(uv_ref, gv_ref, bv_ref, ws_ref, bst_ref, goa_ref, ya_ref, v_ref, *, rows):
    uv = _gelu(uv_ref[...])
    u = uv[:, :GMLP_DIM]
    vr = uv[:, GMLP_DIM:]
    mu = jnp.mean(vr, axis=-1, keepdims=True)
    var = jnp.mean(jnp.square(vr - mu), axis=-1, keepdims=True)
    v = (vr - mu) * lax.rsqrt(var + EPS) * gv_ref[...] + bv_ref[...]
    v_ref[...] = v
    r = lax.broadcasted_iota(I32, (rows, rows), 0)
    c = lax.broadcasted_iota(I32, (rows, rows), 1)
    zs = []
    for g in range(GMLP_GROUPS):
        w = jnp.where(r >= c, ws_ref[g, :rows, :rows], 0.0).astype(BF16)
        vg = v[:, g * GMLP_GROUP_DIM:(g + 1) * GMLP_GROUP_DIM].astype(BF16)
        zs.append(_dot(w, vg) + bst_ref[:rows, g:g + 1])
    y = u * jnp.concatenate(zs, axis=-1)
    ya_ref[...] = _rms(y, goa_ref[...]).astype(BF16)


def _gmlp(uv, g_v, b_v, w_s, b_st, g_out_a, rows):
    n = uv.shape[0]
    row = lambda w: pl.BlockSpec((rows, w), lambda i: (i, 0))
    return pl.pallas_call(
        functools.partial(_gmlp_kernel, rows=rows),
        grid=(n // rows,),
        in_specs=[row(2 * GMLP_DIM), _full((1, GMLP_DIM)), _full((1, GMLP_DIM)), _full(w_s.shape),
                  _full(b_st.shape), _full((1, GMLP_DIM))],
        out_specs=[row(GMLP_DIM), row(GMLP_DIM)],
        out_shape=[jax.ShapeDtypeStruct((n, GMLP_DIM), BF16), jax.ShapeDtypeStruct((n, GMLP_DIM), F32)],
        compiler_params=_cparams(1), name="gmlp",
    )(uv, g_v, b_v, w_s, b_st, g_out_a)


def _mlaq_kernel(cq_ref, gq_ref, wq_ref, wuk_ref, cos_ref, sin_ref, q_ref):
    cq = _rms(cq_ref[...], gq_ref[...])
    q = _dot(cq.astype(BF16), wq_ref[...])
    nr = MLA_HEADS * QK_NOPE
    rw = MLA_HEADS * QK_ROPE
    qrope = q[:, nr:nr + rw] * cos_ref[...] + q[:, nr + rw:] * sin_ref[...]
    for h in range(MLA_HEADS):
        qn = q[:, h * QK_NOPE:(h + 1) * QK_NOPE].astype(BF16)
        q_ref[h, :, :KV_LORA] = _dot(qn, wuk_ref[h]).astype(BF16)
        q_ref[h, :, KV_LORA:] = qrope[:, h * QK_ROPE:(h + 1) * QK_ROPE].astype(BF16)


def _mlaq(c_q, g_q, w_uqp, w_ukt, cos8, sin8, tm):
    n = c_q.shape[0]
    nper = cos8.shape[0] // tm
    tab = pl.BlockSpec((tm, MLA_HEADS * QK_ROPE), lambda i: (i % nper, 0))
    return pl.pallas_call(
        _mlaq_kernel,
        grid=(n // tm,),
        in_specs=[pl.BlockSpec((tm, Q_LORA), lambda i: (i, 0)), _full((1, Q_LORA)), _full(w_uqp.shape),
                  _full(w_ukt.shape), tab, tab],
        out_specs=pl.BlockSpec((MLA_HEADS, tm, QCAT), lambda i: (0, i, 0)),
        out_shape=jax.ShapeDtypeStruct((MLA_HEADS, n, QCAT), BF16),
        compiler_params=_cparams(1), name="mla_q",
    )(c_q, g_q, w_uqp, w_ukt, cos8, sin8)


def _mla_finish(o, wuv_ref, gob_ref, tq):
    ys = [_dot(o[h * tq:(h + 1) * tq].astype(BF16), wuv_ref[h]) for h in range(MLA_HEADS)]
    return _rms(jnp.concatenate(ys, axis=-1), gob_ref[...]).astype(BF16)


def _mla_attn_kernel(q_ref, k_ref, wuv_ref, gob_ref, yb_ref, m_ref, l_ref, acc_ref, *, tq):
    i = pl.program_id(1)
    rows = MLA_HEADS * tq
    q = q_ref[...].reshape(rows, QCAT)
    m_ref[...] = jnp.full((rows, 1), -jnp.inf, F32)
    l_ref[...] = jnp.zeros((rows, 1), F32)
    acc_ref[...] = jnp.zeros((rows, KV_LORA), F32)
    qpos = i * tq + lax.broadcasted_iota(I32, (MLA_HEADS, tq, tq), 1).reshape(rows, tq)
    kofs = lax.broadcasted_iota(I32, (rows, tq), 1)

    def body(j, carry):
        k = k_ref[pl.ds(pl.multiple_of(j * tq, tq), tq), :]
        s = _dot_nt(q, k) * MLA_SCALE
        s = jnp.where(((j * tq + kofs) >> CHUNK_SHIFT) <= (qpos >> CHUNK_SHIFT), s, -jnp.inf)
        m_old = m_ref[...]
        m_new = jnp.maximum(m_old, jnp.max(s, axis=-1, keepdims=True))
        alpha = jnp.exp(m_old - m_new)
        p = jnp.exp(s - m_new)
        l_ref[...] = alpha * l_ref[...] + jnp.sum(p, axis=-1, keepdims=True)
        acc_ref[...] = alpha * acc_ref[...] + _dot(p.astype(BF16), k[:, :KV_LORA])
        m_ref[...] = m_new
        return carry

    lax.fori_loop(0, i + 1, body, 0)
    o = acc_ref[...] / l_ref[...]
    yb_ref[...] = _mla_finish(o, wuv_ref, gob_ref, tq)


def _mla_attn(qcat, kcat, w_uvt, g_out_b, batch, seq, tq):
    n = batch * seq
    nq = seq // tq
    rows = MLA_HEADS * tq
    return pl.pallas_call(
        functools.partial(_mla_attn_kernel, tq=tq),
        grid=(batch, nq),
        in_specs=[pl.BlockSpec((MLA_HEADS, tq, QCAT), lambda b, i: (0, b * nq + i, 0)),
                  pl.BlockSpec((seq, QCAT), lambda b, i: (b, 0)),
                  _full(w_uvt.shape), _full((1, MLA_DIM))],
        out_specs=pl.BlockSpec((tq, MLA_DIM), lambda b, i: (b * nq + i, 0)),
        out_shape=jax.ShapeDtypeStruct((n, MLA_DIM), BF16),
        scratch_shapes=[pltpu.VMEM((rows, 1), F32), pltpu.VMEM((rows, 1), F32),
                        pltpu.VMEM((rows, KV_LORA), F32)],
        compiler_params=_cparams(2), name="mla_attn",
    )(qcat, kcat, w_uvt, g_out_b)


def _mla_dec_kernel(q_ref, cc_ref, ck_ref, kn_ref, wuv_ref, gob_ref, yb_ref, *, t):
    rows = MLA_HEADS * t
    q = q_ref[...].reshape(rows, QCAT)
    cc = cc_ref[0].astype(BF16)
    ck = ck_ref[0].astype(BF16)
    kn = kn_ref[...]
    s_c = (_dot_nt(q[:, :KV_LORA], cc) + _dot_nt(q[:, KV_LORA:], ck)) * MLA_SCALE
    s_n = _dot_nt(q, kn) * MLA_SCALE
    m = jnp.maximum(jnp.max(s_c, axis=-1, keepdims=True), jnp.max(s_n, axis=-1, keepdims=True))
    p_c = jnp.exp(s_c - m)
    p_n = jnp.exp(s_n - m)
    l = jnp.sum(p_c, axis=-1, keepdims=True) + jnp.sum(p_n, axis=-1, keepdims=True)
    o = (_dot(p_c.astype(BF16), cc) + _dot(p_n.astype(BF16), kn[:, :KV_LORA])) / l
    yb_ref[...] = _mla_finish(o, wuv_ref, gob_ref, t)


def _mla_dec(qcat, cache_ckv, cache_krope, kcat, w_uvt, g_out_b, batch, t):
    past = cache_ckv.shape[1]
    return pl.pallas_call(
        functools.partial(_mla_dec_kernel, t=t),
        grid=(batch,),
        in_specs=[pl.BlockSpec((MLA_HEADS, t, QCAT), lambda b: (0, b, 0)),
                  pl.BlockSpec((1, past, KV_LORA), lambda b: (b, 0, 0)),
                  pl.BlockSpec((1, past, QK_ROPE), lambda b: (b, 0, 0)),
                  pl.BlockSpec((t, QCAT), lambda b: (b, 0)),
                  _full(w_uvt.shape), _full((1, MLA_DIM))],
        out_specs=pl.BlockSpec((t, MLA_DIM), lambda b: (b, 0)),
        out_shape=jax.ShapeDtypeStruct((batch * t, MLA_DIM), BF16),
        compiler_params=_cparams(1), name="mla_dec",
    )(qcat, cache_ckv, cache_krope, kcat, w_uvt, g_out_b)


def _memkv_kernel(mem_ref, g_ref, wk_ref, wv_ref, mk_ref, mv_ref):
    mn = _rms(mem_ref[...], g_ref[...]).astype(BF16)
    mk_ref[...] = _dot(mn, wk_ref[...])
    mv_ref[...] = _dot(mn, wv_ref[...])


def _memkv(mem, g_mem, w_mk, w_mv):
    n = mem.shape[0]
    tm = MEM_TOKENS
    return pl.pallas_call(
        _memkv_kernel,
        grid=(n // tm,),
        in_specs=[pl.BlockSpec((tm, D_MODEL), lambda i: (i, 0)), _full((1, D_MODEL)),
                  _full(w_mk.shape), _full(w_mv.shape)],
        out_specs=[pl.BlockSpec((tm, MEM_DIM), lambda i: (i, 0))] * 2,
        out_shape=[jax.ShapeDtypeStruct((n, MEM_DIM), F32)] * 2,
        compiler_params=_cparams(1), name="mem_kv",
    )(mem, g_mem, w_mk, w_mv)


def _memattn_kernel(qm_ref, mk_ref, mv_ref, gom_ref, ym_ref):
    qm = qm_ref[...]
    outs = []
    for h in range(MEM_HEADS):
        sl = slice(h * MEM_HEAD_DIM, (h + 1) * MEM_HEAD_DIM)
        s = _dot_nt(qm[:, sl].astype(BF16), mk_ref[:, sl].astype(BF16)) * MEM_SCALE
        e = jnp.exp(s - jnp.max(s, axis=-1, keepdims=True))
        p = e / jnp.sum(e, axis=-1, keepdims=True)
        outs.append(_dot(p.astype(BF16), mv_ref[:, sl].astype(BF16)))
    ym_ref[...] = _rms(jnp.concatenate(outs, axis=-1), gom_ref[...]).astype(BF16)


def _memattn(q_m, mk, mv, g_out_m, tm, tiles_per_batch):
    n = q_m.shape[0]
    kv = pl.BlockSpec((MEM_TOKENS, MEM_DIM), lambda i: (i // tiles_per_batch, 0))
    return pl.pallas_call(
        _memattn_kernel,
        grid=(n // tm,),
        in_specs=[pl.BlockSpec((tm, MEM_DIM), lambda i: (i, 0)), kv, kv, _full((1, MEM_DIM))],
        out_specs=pl.BlockSpec((tm, MEM_DIM), lambda i: (i, 0)),
        out_shape=jax.ShapeDtypeStruct((n, MEM_DIM), BF16),
        compiler_params=_cparams(1), name="mem_attn",
    )(q_m, mk, mv, g_out_m)


def _merge_kernel(x_ref, ya_ref, yb_ref, ym_ref, w_ref, gf_ref, x1_ref, xf_ref):
    a0, a1 = GMLP_DIM, GMLP_DIM + MLA_DIM
    y = (_dot(ya_ref[...], w_ref[:a0, :]) + _dot(yb_ref[...], w_ref[a0:a1, :])
         + _dot(ym_ref[...], w_ref[a1:, :]))
    x1 = x_ref[...] + y
    x1_ref[...] = x1
    xf_ref[...] = _rms(x1, gf_ref[...])


def _merge(x, ya, yb, ym, w_out, g_ffn, tm):
    n = x.shape[0]
    row = lambda w: pl.BlockSpec((tm, w), lambda i: (i, 0))
    return pl.pallas_call(
        _merge_kernel,
        grid=(n // tm,),
        in_specs=[row(D_MODEL), row(GMLP_DIM), row(MLA_DIM), row(MEM_DIM), _full(w_out.shape),
                  _full((1, D_MODEL))],
        out_specs=[row(D_MODEL), row(D_MODEL)],
        out_shape=[jax.ShapeDtypeStruct((n, D_MODEL), F32)] * 2,
        compiler_params=_cparams(1), name="merge",
    )(x, ya, yb, ym, w_out, g_ffn)


def _top16(s, order, payload):
    vals, picks = [], []
    for _ in range(PEER_TOPK):
        m = jnp.max(s, axis=0, keepdims=True)
        first = jnp.min(jnp.where(s == m, order, BIG_ORDER), axis=0, keepdims=True)
        sel = order == first
        vals.append(m)
        picks.append(first if payload is None else jnp.max(jnp.where(sel, payload, -1.0), axis=0, keepdims=True))
        s = jnp.where(sel, -jnp.inf, s)
    return jnp.concatenate(vals, axis=0), jnp.concatenate(picks, axis=0)


def _pair_candidates(v1, i1, v2, i2):
    tm = v1.shape[1]
    b_iota = lax.broadcasted_iota(I32, (PEER_TOPK, tm), 0).astype(F32)
    cand, eid, order = [], [], []
    for a in range(PAIR_A_FULL):
        cand.append(v1[a:a + 1] + v2)
        eid.append(i1[a:a + 1] * N_KEYS + i2)
        order.append(b_iota + a * PEER_TOPK)
    low_a = b_iota < PAIR_A_FULL
    for b in range(PAIR_B_COLS):
        cand.append(jnp.where(low_a, -jnp.inf, v1 + v2[b:b + 1]))
        eid.append(i1 * N_KEYS + i2[b:b + 1])
        order.append(jnp.where(low_a, BIG_ORDER, b_iota * PEER_TOPK + b))
    return jnp.concatenate(cand, axis=0), jnp.concatenate(order, axis=0), jnp.concatenate(eid, axis=0)


def _peer_topk_kernel(xf_ref, wpq_ref, sk1_ref, sk2_ref, eid_ref, gate_ref, *, tm):
    q = _dot(xf_ref[...].astype(BF16), wpq_ref[...])
    key_order = lax.broadcasted_iota(I32, (N_KEYS, tm), 0).astype(F32)
    for h in range(PEER_HEADS):
        qa = q[:, h * PEER_QDIM:h * PEER_QDIM + PEER_HALF].astype(BF16)
        qb = q[:, h * PEER_QDIM + PEER_HALF:(h + 1) * PEER_QDIM].astype(BF16)
        v1, i1 = _top16(_dot_nt(sk1_ref[...], qa), key_order, None)
        v2, i2 = _top16(_dot_nt(sk2_ref[...], qb), key_order, None)
        vals, eid = _top16(*_pair_candidates(v1, i1, v2, i2))
        e = jnp.exp(vals - vals[0:1])
        rows = slice(h * PEER_TOPK, (h + 1) * PEER_TOPK)
        gate_ref[rows, :] = e / jnp.sum(e, axis=0, keepdims=True)
        eid_ref[rows, :] = eid.astype(I32)


def _peer_topk(xf, w_pq, sk1, sk2, tm):
    n = xf.shape[0]
    out = pl.BlockSpec((PEER_PAIRS, tm), lambda i: (0, i))
    return pl.pallas_call(
        functools.partial(_peer_topk_kernel, tm=tm),
        grid=(n // tm,),
        in_specs=[pl.BlockSpec((tm, D_MODEL), lambda i: (i, 0)), _full(w_pq.shape),
                  _full(sk1.shape), _full(sk2.shape)],
        out_specs=[out, out],
        out_shape=[jax.ShapeDtypeStruct((PEER_PAIRS, n), I32), jax.ShapeDtypeStruct((PEER_PAIRS, n), F32)],
        compiler_params=_cparams(1), name="peer_topk",
    )(xf, w_pq, sk1, sk2)


def _peer_pack_kernel(pu_ref, pv_ref, tab_ref, *, tr):
    for src, base in ((pu_ref, 0), (pv_ref, ROW_CHUNKS // 2)):
        x = src[...]
        words = pltpu.bitcast(pltpu.pack_elementwise([x[:, :HALF_D], x[:, HALF_D:]], packed_dtype=BF16), WORD)
        for c in range(ROW_CHUNKS // 2):
            tab_ref[pl.ds(base + c, tr, stride=ROW_CHUNKS), :] = words[:, c * LANES:(c + 1) * LANES]


def _peer_pack(peer_u, peer_v):
    ne = peer_u.shape[0]
    tr = 256
    return pl.pallas_call(
        functools.partial(_peer_pack_kernel, tr=tr),
        grid=(ne // tr,),
        in_specs=[pl.BlockSpec((tr, D_MODEL), lambda i: (i, 0))] * 2,
        out_specs=pl.BlockSpec((tr * ROW_CHUNKS, LANES), lambda i: (i, 0)),
        out_shape=jax.ShapeDtypeStruct((ne * ROW_CHUNKS, LANES), WORD),
        compiler_params=_cparams(1), name="peer_pack",
    )(peer_u, peer_v)


def _unpack(words):
    return tuple(pltpu.unpack_elementwise(words, index=i, packed_dtype=BF16, unpacked_dtype=F32) for i in (0, 1))


def _peer_mix_kernel(eid_ref, nxt_ref, gate_ref, xf_ref, x1_ref, gfin_ref, tab_ref, y_ref, *scratch, tt):
    bufs, sems = scratch[:MIX_SLOTS], scratch[MIX_SLOTS]
    half_chunks = ROW_CHUNKS // 2
    step, last = pl.program_id(0), pl.num_programs(0) - 1

    def issue(ids_ref, t, k):
        for p in range(PEER_PAIRS):
            row0 = pl.multiple_of(ids_ref[t, p] * ROW_CHUNKS, ROW_CHUNKS)
            pltpu.make_async_copy(tab_ref.at[pl.ds(row0, ROW_CHUNKS)],
                                  bufs[k].at[pl.ds(p * ROW_PITCH, ROW_CHUNKS)],
                                  sems.at[k]).start(priority=p % 2)

    def wait(k):
        n = PEER_PAIRS * ROW_CHUNKS
        pltpu.make_async_copy(tab_ref.at[pl.ds(0, n)], bufs[k].at[pl.ds(0, n)], sems.at[k]).wait()

    def words(k, c):
        return bufs[k][pl.ds(c, PEER_PAIRS, stride=ROW_PITCH), :]

    pair_of_lane = lax.broadcasted_iota(I32, (PEER_PAIRS, 2 * PEER_PAIRS), 1) >> 1
    own_lane = pair_of_lane == lax.broadcasted_iota(I32, (PEER_PAIRS, 2 * PEER_PAIRS), 0)
    sub = lax.broadcasted_iota(I32, (8, 2 * PEER_PAIRS), 0)
    lane = lax.broadcasted_iota(I32, (8, 2 * PEER_PAIRS), 1)
    keep = (sub < 4) & ((lane & 1) == (sub >> 1))
    high_part = (sub & 1) == 0

    def gate_weights(t, k):
        xrow = xf_ref[t]
        acc = jnp.zeros((PEER_PAIRS, LANES), F32)
        for c in range(half_chunks):
            lo, hi = _unpack(words(k, c))
            acc = acc + lo * xrow[:, c * LANES:(c + 1) * LANES]
            acc = acc + hi * xrow[:, HALF_D + c * LANES:HALF_D + (c + 1) * LANES]
        h = jnp.sum(acc, axis=-1, keepdims=True)
        hrow = jnp.sum(jnp.where(own_lane, h, 0.0), axis=0, keepdims=True)
        w = gate_ref[t] * _gelu(hrow)
        w_hi = w.astype(BF16).astype(F32)
        return jnp.where(keep, jnp.where(high_part, w_hi, w - w_hi), 0.0).astype(BF16)

    def mix_values(t, k, lhs):
        rhs = jnp.concatenate([pltpu.bitcast(words(k, half_chunks + c), BF16) for c in range(half_chunks)], axis=-1)
        o = _dot(lhs, rhs)
        yrow = jnp.concatenate([o[0:1] + o[1:2], o[2:3] + o[3:4]], axis=-1)
        y_ref[t] = _rms(x1_ref[t] + yrow, gfin_ref[...])

    @pl.when(step == 0)
    def _():
        for k in range(MIX_AHEAD):
            issue(eid_ref, k, k)

    def group(g, final):
        lhs_prev = None
        for k in range(MIX_SLOTS):
            t = g * MIX_SLOTS + k
            wait(k)
            nxt_k = (k + MIX_AHEAD) % MIX_SLOTS
            if not final or k + MIX_AHEAD < MIX_SLOTS:
                issue(eid_ref, t + MIX_AHEAD, nxt_k)
            else:
                @pl.when(step < last)
                def _():
                    issue(nxt_ref, k + MIX_AHEAD - MIX_SLOTS, nxt_k)
            lhs = gate_weights(t, k)
            if lhs_prev is not None:
                mix_values(t - 1, k - 1, lhs_prev)
            lhs_prev = lhs
        mix_values(g * MIX_SLOTS + MIX_SLOTS - 1, MIX_SLOTS - 1, lhs_prev)

    groups = tt // MIX_SLOTS

    def body(g, carry):
        group(g, False)
        return carry

    lax.fori_loop(0, groups - 1, body, 0)
    group(groups - 1, True)


def _peer_mix(eid_t, gate_t, xf, x1, g_final, table, tt):
    n = xf.shape[0]
    steps = n // tt
    eid = eid_t.T
    gate = jnp.repeat(gate_t.T, 2, axis=1).reshape(n, 1, 2 * PEER_PAIRS)
    row = pl.BlockSpec((tt, 1, D_MODEL), lambda i: (i, 0, 0))
    ids = lambda f: pl.BlockSpec((tt, PEER_PAIRS), f, memory_space=pltpu.SMEM)
    y = pl.pallas_call(
        functools.partial(_peer_mix_kernel, tt=tt),
        grid=(steps,),
        in_specs=[ids(lambda i: (i, 0)), ids(lambda i: (jnp.minimum(i + 1, steps - 1), 0)),
                  pl.BlockSpec((tt, 1, 2 * PEER_PAIRS), lambda i: (i, 0, 0)),
                  row, row, _full((1, D_MODEL)),
                  pl.BlockSpec(memory_space=pl.ANY)],
        out_specs=row,
        out_shape=jax.ShapeDtypeStruct((n, 1, D_MODEL), F32),
        scratch_shapes=[pltpu.VMEM((PEER_PAIRS * ROW_PITCH, LANES), WORD)] * MIX_SLOTS
        + [pltpu.SemaphoreType.DMA((MIX_SLOTS,))],
        compiler_params=_cparams(1), name="peer_mix",
    )(eid, eid, gate, xf.reshape(n, 1, D_MODEL), x1.reshape(n, 1, D_MODEL), g_final, table)
    return y.reshape(n, D_MODEL)


def _rope_tables(pos):
    half = QK_ROPE // 2
    inv = ROPE_THETA ** (-jnp.arange(half, dtype=F32) / half)
    ang = pos.astype(F32)[:, None] * inv[None, :]
    cos, sin = jnp.cos(ang), jnp.sin(ang)
    return jnp.concatenate([cos, cos], -1), jnp.concatenate([sin, sin], -1)


def _rot_cols(w):
    half = w.shape[-1] // 2
    return jnp.concatenate([-w[..., half:], w[..., :half]], axis=-1)


def _group(x, tokens_per_seq, pos, cache, mem_kv, wts, g_final):
    n = x.shape[0]
    nseq = n // tokens_per_seq
    tm = min(256, n)
    cos2, sin2 = _rope_tables(pos)
    reps = max(tm // tokens_per_seq, 1)
    cos2, sin2 = jnp.tile(cos2, (reps, 1)), jnp.tile(sin2, (reps, 1))
    cos8, sin8 = jnp.tile(cos2, (1, MLA_HEADS)), jnp.tile(sin2, (1, MLA_HEADS))

    uv, c_q, ckv, krope, kcat, q_m = _proj(x, wts["g_attn"], wts["w_ext"], wts["g_kv"], cos2, sin2, tm)
    rows = min(GMLP_CHUNK, tokens_per_seq)
    ya, v = _gmlp(uv, wts["g_v"], wts["b_v"], wts["w_s"], wts["b_st"], wts["g_out_a"], rows)
    qcat = _mlaq(c_q, wts["g_q"], wts["w_uqp"], wts["w_ukt"], cos8, sin8, tm)
    if cache is None:
        yb = _mla_attn(qcat, kcat, wts["w_uvt"], wts["g_out_b"], nseq, tokens_per_seq, min(256, tokens_per_seq))
    else:
        yb = _mla_dec(qcat, cache[0], cache[1], kcat, wts["w_uvt"], wts["g_out_b"], nseq, tokens_per_seq)
    mk, mv = mem_kv
    tma = min(tm, tokens_per_seq)
    ym = _memattn(q_m, mk, mv, wts["g_out_m"], tma, tokens_per_seq // tma)
    x1, xf = _merge(x, ya, yb, ym, wts["w_out"], wts["g_ffn"], tm)
    eid_t, gate_t = _peer_topk(xf, wts["w_pq"], wts["sk1"], wts["sk2"], tm)
    y = _peer_mix(eid_t, gate_t, xf, x1, g_final, wts["table"], min(64, n))
    return y, ckv, krope, v


def kernel(x_prompt, x_sample, cache_mla_ckv, cache_mla_krope, cache_mem_k, cache_mem_v, mem_prompt, g_attn, w_in, g_v, b_v, w_s, b_s, g_q, w_uq, w_uk, w_uv, g_kv, g_mem, w_mk, w_mv, g_out_a, g_out_b, g_out_m, w_out, g_ffn, w_pq, sub_keys1, sub_keys2, peer_u, peer_v, g_final):
    assert w_in.shape[0] == 1, "the final norm is fused after the single layer"
    l = 0
    bp, sp, _ = x_prompt.shape
    bs, ts, _ = x_sample.shape
    past = cache_mla_ckv.shape[2]
    gfin = g_final.reshape(1, D_MODEL)
    wi = w_in[l]
    w_ext = jnp.concatenate([wi[:, :_C_KROT], _rot_cols(wi[:, _C_KR:_C_KROT]), wi[:, _C_KROT:]], axis=1)
    wq = w_uq[l].reshape(Q_LORA, MLA_HEADS, QK_NOPE + QK_ROPE)
    wq_rope = wq[:, :, QK_NOPE:]
    w_uqp = jnp.concatenate([wq[:, :, :QK_NOPE].reshape(Q_LORA, -1), wq_rope.reshape(Q_LORA, -1),
                             _rot_cols(wq_rope).reshape(Q_LORA, -1)], axis=1)
    wts = {
        "g_attn": g_attn[l].reshape(1, -1), "w_ext": w_ext.astype(BF16), "g_kv": g_kv[l].reshape(1, -1),
        "g_v": g_v[l].reshape(1, -1), "b_v": b_v[l].reshape(1, -1), "w_s": w_s[l], "b_st": b_s[l].T,
        "g_out_a": g_out_a[l].reshape(1, -1), "g_q": g_q[l].reshape(1, -1), "w_uqp": w_uqp.astype(BF16),
        "w_ukt": jnp.transpose(w_uk[l], (1, 2, 0)).astype(BF16),
        "w_uvt": jnp.transpose(w_uv[l], (1, 0, 2)).astype(BF16),
        "g_out_b": g_out_b[l].reshape(1, -1), "g_out_m": g_out_m[l].reshape(1, -1),
        "w_out": w_out[l].astype(BF16), "g_ffn": g_ffn[l].reshape(1, -1), "w_pq": w_pq[l].astype(BF16),
        "sk1": sub_keys1[l].astype(BF16), "sk2": sub_keys2[l].astype(BF16),
        "table": _peer_pack(peer_u[l], peer_v[l]),
    }
    mk, mv = _memkv(mem_prompt.reshape(bp * MEM_TOKENS, D_MODEL), g_mem[l].reshape(1, -1),
                    w_mk[l].astype(BF16), w_mv[l].astype(BF16))
    yp, ckv_p, kr_p, _ = _group(x_prompt.reshape(bp * sp, D_MODEL), sp, jnp.arange(sp), None, (mk, mv), wts, gfin)
    mem_s = (cache_mem_k[l].reshape(bs * MEM_TOKENS, MEM_DIM), cache_mem_v[l].reshape(bs * MEM_TOKENS, MEM_DIM))
    ys, ckv_s, kr_s, gv_s = _group(x_sample.reshape(bs * ts, D_MODEL), ts, past + jnp.arange(ts),
                                   (cache_mla_ckv[l], cache_mla_krope[l]), mem_s, wts, gfin)
    return (yp.reshape(bp, sp, D_MODEL), ys.reshape(bs, ts, D_MODEL),
            ckv_p.reshape(1, bp, sp, KV_LORA), kr_p.reshape(1, bp, sp, QK_ROPE),
            mk.reshape(1, bp, MEM_TOKENS, MEM_HEADS, MEM_HEAD_DIM),
            mv.reshape(1, bp, MEM_TOKENS, MEM_HEADS, MEM_HEAD_DIM),
            ckv_s.reshape(1, bs, ts, KV_LORA), kr_s.reshape(1, bs, ts, QK_ROPE),
            gv_s.reshape(1, bs, ts, GMLP_DIM))
```

```python
import functools
import math

import jax
import jax.numpy as jnp
from jax import lax
from jax.experimental import pallas as pl
from jax.experimental.pallas import tpu as pltpu

F32 = jnp.float32
BF16 = jnp.bfloat16
I32 = jnp.int32
WORD = jnp.uint32

D_MODEL = 2048
CHUNK = 64
CHUNK_SHIFT = 6
EPS = 1e-6
GMLP_CHUNK = 128
GMLP_DIM = 512
GMLP_GROUPS = 4
GMLP_GROUP_DIM = 128
V_HEAD = 128
QK_NOPE = 128
QK_ROPE = 64
MLA_HEADS = 8
MLA_DIM = 1024
Q_LORA = 512
KV_LORA = 256
ROPE_THETA = 10000.0
MLA_SCALE = (QK_NOPE + QK_ROPE) ** -0.5
QCAT = KV_LORA + QK_ROPE
MEM_TOKENS = 256
MEM_HEADS = 4
MEM_DIM = 512
MEM_HEAD_DIM = 128
MEM_SCALE = MEM_HEAD_DIM ** -0.5
PEER_HEADS = 8
N_KEYS = 128
PEER_QDIM = 256
PEER_HALF = 128
PEER_TOPK = 16
PEER_PAIRS = PEER_HEADS * PEER_TOPK
PAIR_A_FULL = 4
PAIR_B_COLS = 3
BIG_ORDER = 1e9
HALF_D = D_MODEL // 2
LANES = 128
ROW_CHUNKS = 2 * HALF_D // LANES
ROW_PITCH = 24
MIX_SLOTS = 8
MIX_AHEAD = 6

_C_UV, _C_Q, _C_KV, _C_KR, _C_KROT, _C_M, _C_END = 0, 1024, 1536, 1792, 1856, 1920, 2432

VMEM_LIMIT = 48 * 1024 * 1024


def _cparams(n_grid):
    return pltpu.CompilerParams(dimension_semantics=("arbitrary",) * n_grid,
                                vmem_limit_bytes=VMEM_LIMIT)


def _rms(x, g):
    return x * lax.rsqrt(jnp.mean(x * x, axis=-1, keepdims=True) + EPS) * g


def _gelu(x):
    return x * (0.5 * (1.0 + jnp.tanh(math.sqrt(2.0 / math.pi) * (x + 0.044715 * (x * x * x)))))


def _dot(a, b):
    return jnp.dot(a, b, preferred_element_type=F32)


def _dot_nt(a, b):
    return lax.dot_general(a, b, (((1,), (1,)), ((), ())), preferred_element_type=F32)


def _full(shape):
    n = len(shape)
    return pl.BlockSpec(shape, lambda *_: (0,) * n)


def _proj_kernel(x_ref, g_ref, w_ref, gkv_ref, cos_ref, sin_ref,
                 uv_ref, cq_ref, ckv_ref, kr_ref, kcat_ref, qm_ref):
    xn = _rms(x_ref[...], g_ref[...])
    proj = _dot(xn.astype(BF16), w_ref[...])
    uv_ref[...] = proj[:, _C_UV:_C_Q]
    cq_ref[...] = proj[:, _C_Q:_C_KV]
    ckv = _rms(proj[:, _C_KV:_C_KR], gkv_ref[...])
    ckv_ref[...] = ckv
    krope = proj[:, _C_KR:_C_KROT] * cos_ref[...] + proj[:, _C_KROT:_C_M] * sin_ref[...]
    kr_ref[...] = krope
    kcat_ref[:, :KV_LORA] = ckv.astype(BF16)
    kcat_ref[:, KV_LORA:] = krope.astype(BF16)
    qm_ref[...] = proj[:, _C_M:_C_END]


def _proj(x, g_attn, w_ext, g_kv, cos2, sin2, tm):
    n = x.shape[0]
    nper = cos2.shape[0] // tm
    row = lambda w: pl.BlockSpec((tm, w), lambda i: (i, 0))
    tab = pl.BlockSpec((tm, QK_ROPE), lambda i: (i % nper, 0))
    return pl.pallas_call(
        _proj_kernel,
        grid=(n // tm,),
        in_specs=[row(D_MODEL), _full((1, D_MODEL)), _full(w_ext.shape), _full((1, KV_LORA)), tab, tab],
        out_specs=[row(2 * GMLP_DIM), row(Q_LORA), row(KV_LORA), row(QK_ROPE), row(QCAT), row(MEM_DIM)],
        out_shape=[jax.ShapeDtypeStruct((n, 2 * GMLP_DIM), F32), jax.ShapeDtypeStruct((n, Q_LORA), F32),
                   jax.ShapeDtypeStruct((n, KV_LORA), F32), jax.ShapeDtypeStruct((n, QK_ROPE), F32),
                   jax.ShapeDtypeStruct((n, QCAT), BF16), jax.ShapeDtypeStruct((n, MEM_DIM), F32)],
        compiler_params=_cparams(1), name="proj",
    )(x, g_attn, w_ext, g_kv, cos2, sin2)


def _gmlp_kernel(uv_ref, gv_ref, bv_ref, ws_ref, bst_ref, goa_ref, ya_ref, v_ref, *, rows):
    uv = _gelu(uv_ref[...])
    u = uv[:, :GMLP_DIM]
    vr = uv[:, GMLP_DIM:]
    mu = jnp.mean(vr, axis=-1, keepdims=True)
    var = jnp.mean(jnp.square(vr - mu), axis=-1, keepdims=True)
    v = (vr - mu) * lax.rsqrt(var + EPS) * gv_ref[...] + bv_ref[...]
    v_ref[...] = v
    r = lax.broadcasted_iota(I32, (rows, rows), 0)
    c = lax.broadcasted_iota(I32, (rows, rows), 1)
    zs = []
    for g in range(GMLP_GROUPS):
        w = jnp.where(r >= c, ws_ref[g, :rows, :rows], 0.0).astype(BF16)
        vg = v[:, g * GMLP_GROUP_DIM:(g + 1) * GMLP_GROUP_DIM].astype(BF16)
        zs.append(_dot(w, vg) + bst_ref[:rows, g:g + 1])
    y = u * jnp.concatenate(zs, axis=-1)
    ya_ref[...] = _rms(y, goa_ref[...]).astype(BF16)


def _gmlp(uv, g_v, b_v, w_s, b_st, g_out_a, rows):
    n = uv.shape[0]
    row = lambda w: pl.BlockSpec((rows, w), lambda i: (i, 0))
    return pl.pallas_call(
        functools.partial(_gmlp_kernel, rows=rows),
        grid=(n // rows,),
        in_specs=[row(2 * GMLP_DIM), _full((1, GMLP_DIM)), _full((1, GMLP_DIM)), _full(w_s.shape),
                  _full(b_st.shape), _full((1, GMLP_DIM))],
        out_specs=[row(GMLP_DIM), row(GMLP_DIM)],
        out_shape=[jax.ShapeDtypeStruct((n, GMLP_DIM), BF16), jax.ShapeDtypeStruct((n, GMLP_DIM), F32)],
        compiler_params=_cparams(1), name="gmlp",
    )(uv, g_v, b_v, w_s, b_st, g_out_a)


def _mlaq_kernel(cq_ref, gq_ref, wq_ref, wuk_ref, cos_ref, sin_ref, q_ref):
    cq = _rms(cq_ref[...], gq_ref[...])
    q = _dot(cq.astype(BF16), wq_ref[...])
    nr = MLA_HEADS * QK_NOPE
    rw = MLA_HEADS * QK_ROPE
    qrope = q[:, nr:nr + rw] * cos_ref[...] + q[:, nr + rw:] * sin_ref[...]
    for h in range(MLA_HEADS):
        qn = q[:, h * QK_NOPE:(h + 1) * QK_NOPE].astype(BF16)
        q_ref[h, :, :KV_LORA] = _dot(qn, wuk_ref[h]).astype(BF16)
        q_ref[h, :, KV_LORA:] = qrope[:, h * QK_ROPE:(h + 1) * QK_ROPE].astype(BF16)


def _mlaq(c_q, g_q, w_uqp, w_ukt, cos8, sin8, tm):
    n = c_q.shape[0]
    nper = cos8.shape[0] // tm
    tab = pl.BlockSpec((tm, MLA_HEADS * QK_ROPE), lambda i: (i % nper, 0))
    return pl.pallas_call(
        _mlaq_kernel,
        grid=(n // tm,),
        in_specs=[pl.BlockSpec((tm, Q_LORA), lambda i: (i, 0)), _full((1, Q_LORA)), _full(w_uqp.shape),
                  _full(w_ukt.shape), tab, tab],
        out_specs=pl.BlockSpec((MLA_HEADS, tm, QCAT), lambda i: (0, i, 0)),
        out_shape=jax.ShapeDtypeStruct((MLA_HEADS, n, QCAT), BF16),
        compiler_params=_cparams(1), name="mla_q",
    )(c_q, g_q, w_uqp, w_ukt, cos8, sin8)


def _mla_finish(o, wuv_ref, gob_ref, tq):
    ys = [_dot(o[h * tq:(h + 1) * tq].astype(BF16), wuv_ref[h]) for h in range(MLA_HEADS)]
    return _rms(jnp.concatenate(ys, axis=-1), gob_ref[...]).astype(BF16)


def _mla_attn_kernel(q_ref, k_ref, wuvt_ref, gob_ref, yb_ref, m_ref, l_ref, acc_ref, *, tq):
    i = pl.program_id(1)
    cols = MLA_HEADS * tq
    q = q_ref[...].reshape(cols, QCAT)
    m_ref[...] = jnp.full((1, cols), -jnp.inf, F32)
    l_ref[...] = jnp.zeros((1, cols), F32)
    acc_ref[...] = jnp.zeros((KV_LORA, cols), F32)
    key_chunk = lax.broadcasted_iota(I32, (tq, tq), 0) >> CHUNK_SHIFT
    qry_chunk = lax.broadcasted_iota(I32, (tq, tq), 1) >> CHUNK_SHIFT
    allowed = jnp.concatenate([key_chunk <= qry_chunk] * MLA_HEADS, axis=1)

    def block(j, diagonal):
        k = k_ref[pl.ds(pl.multiple_of(j * tq, tq), tq), :]
        v_t = k[:, :KV_LORA].astype(F32).T.astype(BF16)
        s = _dot_nt(k, q) * MLA_SCALE
        if diagonal:
            s = jnp.where(allowed, s, -jnp.inf)
        m_old = m_ref[...]
        m_new = jnp.maximum(m_old, jnp.max(s, axis=0, keepdims=True))
        alpha = jnp.exp(m_old - m_new)
        p = jnp.exp(s - m_new)
        l_ref[...] = alpha * l_ref[...] + jnp.sum(p, axis=0, keepdims=True)
        acc_ref[...] = alpha * acc_ref[...] + _dot(v_t, p.astype(BF16))
        m_ref[...] = m_new

    def body(j, carry):
        block(j, False)
        return carry

    lax.fori_loop(0, i, body, 0)
    block(i, True)
    o = (acc_ref[...] / l_ref[...]).astype(BF16)
    ys = [_dot(wuvt_ref[h], o[:, h * tq:(h + 1) * tq]) for h in range(MLA_HEADS)]
    y = jnp.concatenate(ys, axis=0)
    y = y * lax.rsqrt(jnp.mean(y * y, axis=0, keepdims=True) + EPS) * gob_ref[...]
    yb_ref[...] = y.T.astype(BF16)


def _mla_attn(qcat, kcat, w_uvtt, g_out_b_col, batch, seq, tq):
    n = batch * seq
    nq = seq // tq
    return pl.pallas_call(
        functools.partial(_mla_attn_kernel, tq=tq),
        grid=(batch, nq),
        in_specs=[pl.BlockSpec((MLA_HEADS, tq, QCAT), lambda b, i: (0, b * nq + i, 0)),
                  pl.BlockSpec((seq, QCAT), lambda b, i: (b, 0)),
                  _full(w_uvtt.shape), _full((MLA_DIM, 1))],
        out_specs=pl.BlockSpec((tq, MLA_DIM), lambda b, i: (b * nq + i, 0)),
        out_shape=jax.ShapeDtypeStruct((n, MLA_DIM), BF16),
        scratch_shapes=[pltpu.VMEM((1, MLA_HEADS * tq), F32), pltpu.VMEM((1, MLA_HEADS * tq), F32),
                        pltpu.VMEM((KV_LORA, MLA_HEADS * tq), F32)],
        compiler_params=_cparams(2), name="mla_attn",
    )(qcat, kcat, w_uvtt, g_out_b_col)


def _mla_dec_kernel(q_ref, cc_ref, ck_ref, kn_ref, wuv_ref, gob_ref, yb_ref, *, t):
    rows = MLA_HEADS * t
    q = q_ref[...].reshape(rows, QCAT)
    cc = cc_ref[0].astype(BF16)
    ck = ck_ref[0].astype(BF16)
    kn = kn_ref[...]
    s_c = (_dot_nt(q[:, :KV_LORA], cc) + _dot_nt(q[:, KV_LORA:], ck)) * MLA_SCALE
    s_n = _dot_nt(q, kn) * MLA_SCALE
    m = jnp.maximum(jnp.max(s_c, axis=-1, keepdims=True), jnp.max(s_n, axis=-1, keepdims=True))
    p_c = jnp.exp(s_c - m)
    p_n = jnp.exp(s_n - m)
    l = jnp.sum(p_c, axis=-1, keepdims=True) + jnp.sum(p_n, axis=-1, keepdims=True)
    o = (_dot(p_c.astype(BF16), cc) + _dot(p_n.astype(BF16), kn[:, :KV_LORA])) / l
    yb_ref[...] = _mla_finish(o, wuv_ref, gob_ref, t)


def _mla_dec(qcat, cache_ckv, cache_krope, kcat, w_uvt, g_out_b, batch, t):
    past = cache_ckv.shape[1]
    return pl.pallas_call(
        functools.partial(_mla_dec_kernel, t=t),
        grid=(batch,),
        in_specs=[pl.BlockSpec((MLA_HEADS, t, QCAT), lambda b: (0, b, 0)),
                  pl.BlockSpec((1, past, KV_LORA), lambda b: (b, 0, 0)),
                  pl.BlockSpec((1, past, QK_ROPE), lambda b: (b, 0, 0)),
                  pl.BlockSpec((t, QCAT), lambda b: (b, 0)),
                  _full(w_uvt.shape), _full((1, MLA_DIM))],
        out_specs=pl.BlockSpec((t, MLA_DIM), lambda b: (b, 0)),
        out_shape=jax.ShapeDtypeStruct((batch * t, MLA_DIM), BF16),
        compiler_params=_cparams(1), name="mla_dec",
    )(qcat, cache_ckv, cache_krope, kcat, w_uvt, g_out_b)


def _memkv_kernel(mem_ref, g_ref, wk_ref, wv_ref, mk_ref, mv_ref):
    mn = _rms(mem_ref[...], g_ref[...]).astype(BF16)
    mk_ref[...] = _dot(mn, wk_ref[...])
    mv_ref[...] = _dot(mn, wv_ref[...])


def _memkv(mem, g_mem, w_mk, w_mv):
    n = mem.shape[0]
    tm = MEM_TOKENS
    return pl.pallas_call(
        _memkv_kernel,
        grid=(n // tm,),
        in_specs=[pl.BlockSpec((tm, D_MODEL), lambda i: (i, 0)), _full((1, D_MODEL)),
                  _full(w_mk.shape), _full(w_mv.shape)],
        out_specs=[pl.BlockSpec((tm, MEM_DIM), lambda i: (i, 0))] * 2,
        out_shape=[jax.ShapeDtypeStruct((n, MEM_DIM), F32)] * 2,
        compiler_params=_cparams(1), name="mem_kv",
    )(mem, g_mem, w_mk, w_mv)


def _memattn_kernel(qm_ref, mk_ref, mv_ref, gom_ref, ym_ref):
    qm = qm_ref[...]
    outs = []
    for h in range(MEM_HEADS):
        sl = slice(h * MEM_HEAD_DIM, (h + 1) * MEM_HEAD_DIM)
        s = _dot_nt(qm[:, sl].astype(BF16), mk_ref[:, sl].astype(BF16)) * MEM_SCALE
        e = jnp.exp(s - jnp.max(s, axis=-1, keepdims=True))
        p = e / jnp.sum(e, axis=-1, keepdims=True)
        outs.append(_dot(p.astype(BF16), mv_ref[:, sl].astype(BF16)))
    ym_ref[...] = _rms(jnp.concatenate(outs, axis=-1), gom_ref[...]).astype(BF16)


def _memattn(q_m, mk, mv, g_out_m, tm, tiles_per_batch):
    n = q_m.shape[0]
    kv = pl.BlockSpec((MEM_TOKENS, MEM_DIM), lambda i: (i // tiles_per_batch, 0))
    return pl.pallas_call(
        _memattn_kernel,
        grid=(n // tm,),
        in_specs=[pl.BlockSpec((tm, MEM_DIM), lambda i: (i, 0)), kv, kv, _full((1, MEM_DIM))],
        out_specs=pl.BlockSpec((tm, MEM_DIM), lambda i: (i, 0)),
        out_shape=jax.ShapeDtypeStruct((n, MEM_DIM), BF16),
        compiler_params=_cparams(1), name="mem_attn",
    )(q_m, mk, mv, g_out_m)


def _merge_kernel(x_ref, ya_ref, yb_ref, ym_ref, w_ref, gf_ref, x1_ref, xf_ref):
    a0, a1 = GMLP_DIM, GMLP_DIM + MLA_DIM
    y = (_dot(ya_ref[...], w_ref[:a0, :]) + _dot(yb_ref[...], w_ref[a0:a1, :])
         + _dot(ym_ref[...], w_ref[a1:, :]))
    x1 = x_ref[...] + y
    x1_ref[...] = x1
    xf_ref[...] = _rms(x1, gf_ref[...])


def _merge(x, ya, yb, ym, w_out, g_ffn, tm):
    n = x.shape[0]
    row = lambda w: pl.BlockSpec((tm, w), lambda i: (i, 0))
    return pl.pallas_call(
        _merge_kernel,
        grid=(n // tm,),
        in_specs=[row(D_MODEL), row(GMLP_DIM), row(MLA_DIM), row(MEM_DIM), _full(w_out.shape),
                  _full((1, D_MODEL))],
        out_specs=[row(D_MODEL), row(D_MODEL)],
        out_shape=[jax.ShapeDtypeStruct((n, D_MODEL), F32)] * 2,
        compiler_params=_cparams(1), name="merge",
    )(x, ya, yb, ym, w_out, g_ffn)


def _top16(s, order, payload):
    vals, picks = [], []
    for _ in range(PEER_TOPK):
        m = jnp.max(s, axis=0, keepdims=True)
        first = jnp.min(jnp.where(s == m, order, BIG_ORDER), axis=0, keepdims=True)
        sel = order == first
        vals.append(m)
        picks.append(first if payload is None else jnp.max(jnp.where(sel, payload, -1.0), axis=0, keepdims=True))
        s = jnp.where(sel, -jnp.inf, s)
    return jnp.concatenate(vals, axis=0), jnp.concatenate(picks, axis=0)


def _pair_candidates(v1, i1, v2, i2):
    tm = v1.shape[1]
    b_iota = lax.broadcasted_iota(I32, (PEER_TOPK, tm), 0).astype(F32)
    cand, eid, order = [], [], []
    for a in range(PAIR_A_FULL):
        cand.append(v1[a:a + 1] + v2)
        eid.append(i1[a:a + 1] * N_KEYS + i2)
        order.append(b_iota + a * PEER_TOPK)
    low_a = b_iota < PAIR_A_FULL
    for b in range(PAIR_B_COLS):
        cand.append(jnp.where(low_a, -jnp.inf, v1 + v2[b:b + 1]))
        eid.append(i1 * N_KEYS + i2[b:b + 1])
        order.append(jnp.where(low_a, BIG_ORDER, b_iota * PEER_TOPK + b))
    return jnp.concatenate(cand, axis=0), jnp.concatenate(order, axis=0), jnp.concatenate(eid, axis=0)


def _peer_topk_kernel(xf_ref, wpq_ref, sk1_ref, sk2_ref, eid_ref, gate_ref, *, tm):
    q = _dot(xf_ref[...].astype(BF16), wpq_ref[...])
    key_order = lax.broadcasted_iota(I32, (N_KEYS, tm), 0).astype(F32)
    for h in range(PEER_HEADS):
        qa = q[:, h * PEER_QDIM:h * PEER_QDIM + PEER_HALF].astype(BF16)
        qb = q[:, h * PEER_QDIM + PEER_HALF:(h + 1) * PEER_QDIM].astype(BF16)
        v1, i1 = _top16(_dot_nt(sk1_ref[...], qa), key_order, None)
        v2, i2 = _top16(_dot_nt(sk2_ref[...], qb), key_order, None)
        vals, eid = _top16(*_pair_candidates(v1, i1, v2, i2))
        e = jnp.exp(vals - vals[0:1])
        rows = slice(h * PEER_TOPK, (h + 1) * PEER_TOPK)
        gate_ref[rows, :] = e / jnp.sum(e, axis=0, keepdims=True)
        eid_ref[rows, :] = eid.astype(I32)


def _peer_topk(xf, w_pq, sk1, sk2, tm):
    n = xf.shape[0]
    out = pl.BlockSpec((PEER_PAIRS, tm), lambda i: (0, i))
    return pl.pallas_call(
        functools.partial(_peer_topk_kernel, tm=tm),
        grid=(n // tm,),
        in_specs=[pl.BlockSpec((tm, D_MODEL), lambda i: (i, 0)), _full(w_pq.shape),
                  _full(sk1.shape), _full(sk2.shape)],
        out_specs=[out, out],
        out_shape=[jax.ShapeDtypeStruct((PEER_PAIRS, n), I32), jax.ShapeDtypeStruct((PEER_PAIRS, n), F32)],
        compiler_params=_cparams(1), name="peer_topk",
    )(xf, w_pq, sk1, sk2)


def _peer_pack_kernel(pu_ref, pv_ref, tab_ref, *, tr):
    for src, base in ((pu_ref, 0), (pv_ref, ROW_CHUNKS // 2)):
        x = src[...]
        words = pltpu.bitcast(pltpu.pack_elementwise([x[:, :HALF_D], x[:, HALF_D:]], packed_dtype=BF16), WORD)
        for c in range(ROW_CHUNKS // 2):
            tab_ref[pl.ds(base + c, tr, stride=ROW_CHUNKS), :] = words[:, c * LANES:(c + 1) * LANES]


def _peer_pack(peer_u, peer_v):
    ne = peer_u.shape[0]
    tr = 256
    return pl.pallas_call(
        functools.partial(_peer_pack_kernel, tr=tr),
        grid=(ne // tr,),
        in_specs=[pl.BlockSpec((tr, D_MODEL), lambda i: (i, 0))] * 2,
        out_specs=pl.BlockSpec((tr * ROW_CHUNKS, LANES), lambda i: (i, 0)),
        out_shape=jax.ShapeDtypeStruct((ne * ROW_CHUNKS, LANES), WORD),
        compiler_params=_cparams(1), name="peer_pack",
    )(peer_u, peer_v)


def _unpack(words):
    return tuple(pltpu.unpack_elementwise(words, index=i, packed_dtype=BF16, unpacked_dtype=F32) for i in (0, 1))


def _peer_mix_kernel(eid_ref, nxt_ref, gate_ref, xf_ref, x1_ref, gfin_ref, tab_ref, y_ref, *scratch, tt):
    bufs, sems = scratch[:MIX_SLOTS], scratch[MIX_SLOTS]
    half_chunks = ROW_CHUNKS // 2
    step, last = pl.program_id(0), pl.num_programs(0) - 1

    def issue(ids_ref, t, k):
        for p in range(PEER_PAIRS):
            row0 = pl.multiple_of(ids_ref[t, p] * ROW_CHUNKS, ROW_CHUNKS)
            pltpu.make_async_copy(tab_ref.at[pl.ds(row0, ROW_CHUNKS)],
                                  bufs[k].at[pl.ds(p * ROW_PITCH, ROW_CHUNKS)],
                                  sems.at[k]).start(priority=p % 2)

    def wait(k):
        n = PEER_PAIRS * ROW_CHUNKS
        pltpu.make_async_copy(tab_ref.at[pl.ds(0, n)], bufs[k].at[pl.ds(0, n)], sems.at[k]).wait()

    def words(k, c):
        return bufs[k][pl.ds(c, PEER_PAIRS, stride=ROW_PITCH), :]

    pair_of_lane = lax.broadcasted_iota(I32, (PEER_PAIRS, 2 * PEER_PAIRS), 1) >> 1
    own_lane = pair_of_lane == lax.broadcasted_iota(I32, (PEER_PAIRS, 2 * PEER_PAIRS), 0)
    sub = lax.broadcasted_iota(I32, (8, 2 * PEER_PAIRS), 0)
    lane = lax.broadcasted_iota(I32, (8, 2 * PEER_PAIRS), 1)
    keep = (sub < 4) & ((lane & 1) == (sub >> 1))
    high_part = (sub & 1) == 0

    def gate_weights(k, xrow, grow):
        acc = jnp.zeros((PEER_PAIRS, LANES), F32)
        for c in range(half_chunks):
            lo, hi = _unpack(words(k, c))
            acc = acc + lo * xrow[:, c * LANES:(c + 1) * LANES]
            acc = acc + hi * xrow[:, HALF_D + c * LANES:HALF_D + (c + 1) * LANES]
        h = jnp.sum(acc, axis=-1, keepdims=True)
        hrow = jnp.sum(jnp.where(own_lane, h, 0.0), axis=0, keepdims=True)
        w = grow * _gelu(hrow)
        w_hi = w.astype(BF16).astype(F32)
        return jnp.where(keep, jnp.where(high_part, w_hi, w - w_hi), 0.0).astype(BF16)

    def mix_values(k, lhs):
        rhs = jnp.concatenate([pltpu.bitcast(words(k, half_chunks + c), BF16) for c in range(half_chunks)], axis=-1)
        o = _dot(lhs, rhs)
        return jnp.concatenate([o[0:1] + o[1:2], o[2:3] + o[3:4]], axis=-1)

    @pl.when(step == 0)
    def _():
        for k in range(MIX_AHEAD):
            issue(eid_ref, k, k)

    def group(g, final):
        rows = pl.ds(pl.multiple_of(g * MIX_SLOTS, MIX_SLOTS), MIX_SLOTS)
        x_tile, g_tile = xf_ref[rows, :], gate_ref[rows, :]
        lhs_prev, yrows = None, []
        for k in range(MIX_SLOTS):
            wait(k)
            nxt_k = (k + MIX_AHEAD) % MIX_SLOTS
            if not final or k + MIX_AHEAD < MIX_SLOTS:
                issue(eid_ref, g * MIX_SLOTS + k + MIX_AHEAD, nxt_k)
            else:
                @pl.when(step < last)
                def _():
                    issue(nxt_ref, k + MIX_AHEAD - MIX_SLOTS, nxt_k)
            lhs = gate_weights(k, x_tile[k:k + 1], g_tile[k:k + 1])
            if lhs_prev is not None:
                yrows.append(mix_values(k - 1, lhs_prev))
            lhs_prev = lhs
        yrows.append(mix_values(MIX_SLOTS - 1, lhs_prev))
        y_ref[rows, :] = _rms(x1_ref[rows, :] + jnp.concatenate(yrows, axis=0), gfin_ref[...])

    groups = tt // MIX_SLOTS

    def body(g, carry):
        group(g, False)
        return carry

    lax.fori_loop(0, groups - 1, body, 0)
    group(groups - 1, True)


def _peer_mix(eid_t, gate_t, xf, x1, g_final, table, tt):
    n = xf.shape[0]
    steps = n // tt
    eid = eid_t.T
    gate = jnp.repeat(gate_t.T, 2, axis=1)
    row = pl.BlockSpec((tt, D_MODEL), lambda i: (i, 0))
    ids = lambda f: pl.BlockSpec((tt, PEER_PAIRS), f, memory_space=pltpu.SMEM)
    assert MIX_SLOTS == 8 and tt % MIX_SLOTS == 0
    return pl.pallas_call(
        functools.partial(_peer_mix_kernel, tt=tt),
        grid=(steps,),
        in_specs=[ids(lambda i: (i, 0)), ids(lambda i: (jnp.minimum(i + 1, steps - 1), 0)),
                  pl.BlockSpec((tt, 2 * PEER_PAIRS), lambda i: (i, 0)),
                  row, row, _full((1, D_MODEL)),
                  pl.BlockSpec(memory_space=pl.ANY)],
        out_specs=row,
        out_shape=jax.ShapeDtypeStruct((n, D_MODEL), F32),
        scratch_shapes=[pltpu.VMEM((PEER_PAIRS * ROW_PITCH, LANES), WORD)] * MIX_SLOTS
        + [pltpu.SemaphoreType.DMA((MIX_SLOTS,))],
        compiler_params=_cparams(1), name="peer_mix",
    )(eid, eid, gate, xf, x1, g_final, table)


def _rope_tables(pos):
    half = QK_ROPE // 2
    inv = ROPE_THETA ** (-jnp.arange(half, dtype=F32) / half)
    ang = pos.astype(F32)[:, None] * inv[None, :]
    cos, sin = jnp.cos(ang), jnp.sin(ang)
    return jnp.concatenate([cos, cos], -1), jnp.concatenate([sin, sin], -1)


def _rot_cols(w):
    half = w.shape[-1] // 2
    return jnp.concatenate([-w[..., half:], w[..., :half]], axis=-1)


def _group(x, tokens_per_seq, pos, cache, mem_kv, wts, g_final):
    n = x.shape[0]
    nseq = n // tokens_per_seq
    tm = min(256, n)
    cos2, sin2 = _rope_tables(pos)
    reps = max(tm // tokens_per_seq, 1)
    cos2, sin2 = jnp.tile(cos2, (reps, 1)), jnp.tile(sin2, (reps, 1))
    cos8, sin8 = jnp.tile(cos2, (1, MLA_HEADS)), jnp.tile(sin2, (1, MLA_HEADS))

    uv, c_q, ckv, krope, kcat, q_m = _proj(x, wts["g_attn"], wts["w_ext"], wts["g_kv"], cos2, sin2, tm)
    rows = min(GMLP_CHUNK, tokens_per_seq)
    ya, v = _gmlp(uv, wts["g_v"], wts["b_v"], wts["w_s"], wts["b_st"], wts["g_out_a"], rows)
    qcat = _mlaq(c_q, wts["g_q"], wts["w_uqp"], wts["w_ukt"], cos8, sin8, tm)
    if cache is None:
        yb = _mla_attn(qcat, kcat, jnp.swapaxes(wts["w_uvt"], 1, 2), wts["g_out_b"].reshape(MLA_DIM, 1), nseq,
                       tokens_per_seq, min(256, tokens_per_seq))
    else:
        yb = _mla_dec(qcat, cache[0], cache[1], kcat, wts["w_uvt"], wts["g_out_b"], nseq, tokens_per_seq)
    mk, mv = mem_kv
    tma = min(tm, tokens_per_seq)
    ym = _memattn(q_m, mk, mv, wts["g_out_m"], tma, tokens_per_seq // tma)
    x1, xf = _merge(x, ya, yb, ym, wts["w_out"], wts["g_ffn"], tm)
    eid_t, gate_t = _peer_topk(xf, wts["w_pq"], wts["sk1"], wts["sk2"], tm)
    y = _peer_mix(eid_t, gate_t, xf, x1, g_final, wts["table"], min(64, n))
    return y, ckv, krope, v


def kernel(x_prompt, x_sample, cache_mla_ckv, cache_mla_krope, cache_mem_k, cache_mem_v, mem_prompt, g_attn, w_in, g_v, b_v, w_s, b_s, g_q, w_uq, w_uk, w_uv, g_kv, g_mem, w_mk, w_mv, g_out_a, g_out_b, g_out_m, w_out, g_ffn, w_pq, sub_keys1, sub_keys2, peer_u, peer_v, g_final):
    assert w_in.shape[0] == 1, "the final norm is fused after the single layer"
    l = 0
    bp, sp, _ = x_prompt.shape
    bs, ts, _ = x_sample.shape
    past = cache_mla_ckv.shape[2]
    gfin = g_final.reshape(1, D_MODEL)
    wi = w_in[l]
    w_ext = jnp.concatenate([wi[:, :_C_KROT], _rot_cols(wi[:, _C_KR:_C_KROT]), wi[:, _C_KROT:]], axis=1)
    wq = w_uq[l].reshape(Q_LORA, MLA_HEADS, QK_NOPE + QK_ROPE)
    wq_rope = wq[:, :, QK_NOPE:]
    w_uqp = jnp.concatenate([wq[:, :, :QK_NOPE].reshape(Q_LORA, -1), wq_rope.reshape(Q_LORA, -1),
                             _rot_cols(wq_rope).reshape(Q_LORA, -1)], axis=1)
    wts = {
        "g_attn": g_attn[l].reshape(1, -1), "w_ext": w_ext.astype(BF16), "g_kv": g_kv[l].reshape(1, -1),
        "g_v": g_v[l].reshape(1, -1), "b_v": b_v[l].reshape(1, -1), "w_s": w_s[l], "b_st": b_s[l].T,
        "g_out_a": g_out_a[l].reshape(1, -1), "g_q": g_q[l].reshape(1, -1), "w_uqp": w_uqp.astype(BF16),
        "w_ukt": jnp.transpose(w_uk[l], (1, 2, 0)).astype(BF16),
        "w_uvt": jnp.transpose(w_uv[l], (1, 0, 2)).astype(BF16),
        "g_out_b": g_out_b[l].reshape(1, -1), "g_out_m": g_out_m[l].reshape(1, -1),
        "w_out": w_out[l].astype(BF16), "g_ffn": g_ffn[l].reshape(1, -1), "w_pq": w_pq[l].astype(BF16),
        "sk1": sub_keys1[l].astype(BF16), "sk2": sub_keys2[l].astype(BF16),
        "table": _peer_pack(peer_u[l], peer_v[l]),
    }
    mk, mv = _memkv(mem_prompt.reshape(bp * MEM_TOKENS, D_MODEL), g_mem[l].reshape(1, -1),
                    w_mk[l].astype(BF16), w_mv[l].astype(BF16))
    yp, ckv_p, kr_p, _ = _group(x_prompt.reshape(bp * sp, D_MODEL), sp, jnp.arange(sp), None, (mk, mv), wts, gfin)
    mem_s = (cache_mem_k[l].reshape(bs * MEM_TOKENS, MEM_DIM), cache_mem_v[l].reshape(bs * MEM_TOKENS, MEM_DIM))
    ys, ckv_s, kr_s, gv_s = _group(x_sample.reshape(bs * ts, D_MODEL), ts, past + jnp.arange(ts),
                                   (cache_mla_ckv[l], cache_mla_krope[l]), mem_s, wts, gfin)
    return (yp.reshape(bp, sp, D_MODEL), ys.reshape(bs, ts, D_MODEL),
            ckv_p.reshape(1, bp, sp, KV_LORA), kr_p.reshape(1, bp, sp, QK_ROPE),
            mk.reshape(1, bp, MEM_TOKENS, MEM_HEADS, MEM_HEAD_DIM),
            mv.reshape(1, bp, MEM_TOKENS, MEM_HEADS, MEM_HEAD_DIM),
            ckv_s.reshape(1, bs, ts, KV_LORA), kr_s.reshape(1, bs, ts, QK_ROPE),
            gv_s.reshape(1, bs, ts, GMLP_DIM))
```

```python
import functools
import math

import jax
import jax.numpy as jnp
from jax import lax
from jax.experimental import pallas as pl
from jax.experimental.pallas import tpu as pltpu

F32 = jnp.float32
BF16 = jnp.bfloat16
I32 = jnp.int32
WORD = jnp.uint32

D_MODEL = 2048
CHUNK = 64
CHUNK_SHIFT = 6
EPS = 1e-6
GMLP_CHUNK = 128
GMLP_DIM = 512
GMLP_GROUPS = 4
GMLP_GROUP_DIM = 128
V_HEAD = 128
QK_NOPE = 128
QK_ROPE = 64
MLA_HEADS = 8
MLA_DIM = 1024
Q_LORA = 512
KV_LORA = 256
ROPE_THETA = 10000.0
MLA_SCALE = (QK_NOPE + QK_ROPE) ** -0.5
QCAT = KV_LORA + QK_ROPE
MEM_TOKENS = 256
MEM_HEADS = 4
MEM_DIM = 512
MEM_HEAD_DIM = 128
MEM_SCALE = MEM_HEAD_DIM ** -0.5
PEER_HEADS = 8
N_KEYS = 128
PEER_QDIM = 256
PEER_HALF = 128
PEER_TOPK = 16
PEER_PAIRS = PEER_HEADS * PEER_TOPK
PAIR_A_FULL = 4
PAIR_B_COLS = 3
BIG_ORDER = 1e9
HALF_D = D_MODEL // 2
LANES = 128
ROW_CHUNKS = 2 * HALF_D // LANES
ROW_PITCH = 17
MIX_SLOTS = 8
MIX_AHEAD = 6

_C_UV, _C_Q, _C_KV, _C_KR, _C_KROT, _C_M, _C_END = 0, 1024, 1536, 1792, 1856, 1920, 2432

VMEM_LIMIT = 48 * 1024 * 1024


def _cparams(n_grid):
    return pltpu.CompilerParams(dimension_semantics=("arbitrary",) * n_grid,
                                vmem_limit_bytes=VMEM_LIMIT)


def _rms(x, g):
    return x * lax.rsqrt(jnp.mean(x * x, axis=-1, keepdims=True) + EPS) * g


def _gelu(x):
    return x * (0.5 * (1.0 + jnp.tanh(math.sqrt(2.0 / math.pi) * (x + 0.044715 * (x * x * x)))))


def _dot(a, b):
    return jnp.dot(a, b, preferred_element_type=F32)


def _dot_nt(a, b):
    return lax.dot_general(a, b, (((1,), (1,)), ((), ())), preferred_element_type=F32)


def _full(shape):
    n = len(shape)
    return pl.BlockSpec(shape, lambda *_: (0,) * n)


def _proj_kernel(x_ref, g_ref, w_ref, gkv_ref, cos_ref, sin_ref,
                 uv_ref, cq_ref, ckv_ref, kr_ref, kcat_ref, qm_ref):
    xn = _rms(x_ref[...], g_ref[...])
    proj = _dot(xn.astype(BF16), w_ref[...])
    uv_ref[...] = proj[:, _C_UV:_C_Q]
    cq_ref[...] = proj[:, _C_Q:_C_KV]
    ckv = _rms(proj[:, _C_KV:_C_KR], gkv_ref[...])
    ckv_ref[...] = ckv
    krope = proj[:, _C_KR:_C_KROT] * cos_ref[...] + proj[:, _C_KROT:_C_M] * sin_ref[...]
    kr_ref[...] = krope
    kcat_ref[:, :KV_LORA] = ckv.astype(BF16)
    kcat_ref[:, KV_LORA:] = krope.astype(BF16)
    qm_ref[...] = proj[:, _C_M:_C_END]


def _proj(x, g_attn, w_ext, g_kv, cos2, sin2, tm):
    n = x.shape[0]
    nper = cos2.shape[0] // tm
    row = lambda w: pl.BlockSpec((tm, w), lambda i: (i, 0))
    tab = pl.BlockSpec((tm, QK_ROPE), lambda i: (i % nper, 0))
    return pl.pallas_call(
        _proj_kernel,
        grid=(n // tm,),
        in_specs=[row(D_MODEL), _full((1, D_MODEL)), _full(w_ext.shape), _full((1, KV_LORA)), tab, tab],
        out_specs=[row(2 * GMLP_DIM), row(Q_LORA), row(KV_LORA), row(QK_ROPE), row(QCAT), row(MEM_DIM)],
        out_shape=[jax.ShapeDtypeStruct((n, 2 * GMLP_DIM), F32), jax.ShapeDtypeStruct((n, Q_LORA), F32),
                   jax.ShapeDtypeStruct((n, KV_LORA), F32), jax.ShapeDtypeStruct((n, QK_ROPE), F32),
                   jax.ShapeDtypeStruct((n, QCAT), BF16), jax.ShapeDtypeStruct((n, MEM_DIM), F32)],
        compiler_params=_cparams(1), name="proj",
    )(x, g_attn, w_ext, g_kv, cos2, sin2)


def _gmlp_kernel(uv_ref, gv_ref, bv_ref, ws_ref, bst_ref, goa_ref, ya_ref, v_ref, *, rows):
    uv = _gelu(uv_ref[...])
    u = uv[:, :GMLP_DIM]
    vr = uv[:, GMLP_DIM:]
    mu = jnp.mean(vr, axis=-1, keepdims=True)
    var = jnp.mean(jnp.square(vr - mu), axis=-1, keepdims=True)
    v = (vr - mu) * lax.rsqrt(var + EPS) * gv_ref[...] + bv_ref[...]
    v_ref[...] = v
    r = lax.broadcasted_iota(I32, (rows, rows), 0)
    c = lax.broadcasted_iota(I32, (rows, rows), 1)
    zs = []
    for g in range(GMLP_GROUPS):
        w = jnp.where(r >= c, ws_ref[g, :rows, :rows], 0.0).astype(BF16)
        vg = v[:, g * GMLP_GROUP_DIM:(g + 1) * GMLP_GROUP_DIM].astype(BF16)
        zs.append(_dot(w, vg) + bst_ref[:rows, g:g + 1])
    y = u * jnp.concatenate(zs, axis=-1)
    ya_ref[...] = _rms(y, goa_ref[...]).astype(BF16)


def _gmlp(uv, g_v, b_v, w_s, b_st, g_out_a, rows):
    n = uv.shape[0]
    row = lambda w: pl.BlockSpec((rows, w), lambda i: (i, 0))
    return pl.pallas_call(
        functools.partial(_gmlp_kernel, rows=rows),
        grid=(n // rows,),
        in_specs=[row(2 * GMLP_DIM), _full((1, GMLP_DIM)), _full((1, GMLP_DIM)), _full(w_s.shape),
                  _full(b_st.shape), _full((1, GMLP_DIM))],
        out_specs=[row(GMLP_DIM), row(GMLP_DIM)],
        out_shape=[jax.ShapeDtypeStruct((n, GMLP_DIM), BF16), jax.ShapeDtypeStruct((n, GMLP_DIM), F32)],
        compiler_params=_cparams(1), name="gmlp",
    )(uv, g_v, b_v, w_s, b_st, g_out_a)


def _mlaq_kernel(cq_ref, gq_ref, wq_ref, wuk_ref, cos_ref, sin_ref, q_ref):
    cq = _rms(cq_ref[...], gq_ref[...])
    q = _dot(cq.astype(BF16), wq_ref[...])
    nr = MLA_HEADS * QK_NOPE
    rw = MLA_HEADS * QK_ROPE
    qrope = q[:, nr:nr + rw] * cos_ref[...] + q[:, nr + rw:] * sin_ref[...]
    for h in range(MLA_HEADS):
        qn = q[:, h * QK_NOPE:(h + 1) * QK_NOPE].astype(BF16)
        q_ref[h, :, :KV_LORA] = _dot(qn, wuk_ref[h]).astype(BF16)
        q_ref[h, :, KV_LORA:] = qrope[:, h * QK_ROPE:(h + 1) * QK_ROPE].astype(BF16)


def _mlaq(c_q, g_q, w_uqp, w_ukt, cos8, sin8, tm):
    n = c_q.shape[0]
    nper = cos8.shape[0] // tm
    tab = pl.BlockSpec((tm, MLA_HEADS * QK_ROPE), lambda i: (i % nper, 0))
    return pl.pallas_call(
        _mlaq_kernel,
        grid=(n // tm,),
        in_specs=[pl.BlockSpec((tm, Q_LORA), lambda i: (i, 0)), _full((1, Q_LORA)), _full(w_uqp.shape),
                  _full(w_ukt.shape), tab, tab],
        out_specs=pl.BlockSpec((MLA_HEADS, tm, QCAT), lambda i: (0, i, 0)),
        out_shape=jax.ShapeDtypeStruct((MLA_HEADS, n, QCAT), BF16),
        compiler_params=_cparams(1), name="mla_q",
    )(c_q, g_q, w_uqp, w_ukt, cos8, sin8)


def _mla_finish(o, wuv_ref, gob_ref, tq):
    ys = [_dot(o[h * tq:(h + 1) * tq].astype(BF16), wuv_ref[h]) for h in range(MLA_HEADS)]
    return _rms(jnp.concatenate(ys, axis=-1), gob_ref[...]).astype(BF16)


def _mla_attn_kernel(q_ref, k_ref, wuvt_ref, gob_ref, yb_ref, m_ref, l_ref, acc_ref, *, tq):
    i = pl.program_id(1)
    cols = MLA_HEADS * tq
    q = q_ref[...].reshape(cols, QCAT)
    m_ref[...] = jnp.full((1, cols), -jnp.inf, F32)
    l_ref[...] = jnp.zeros((1, cols), F32)
    acc_ref[...] = jnp.zeros((KV_LORA, cols), F32)
    key_chunk = lax.broadcasted_iota(I32, (tq, tq), 0) >> CHUNK_SHIFT
    qry_chunk = lax.broadcasted_iota(I32, (tq, tq), 1) >> CHUNK_SHIFT
    allowed = jnp.concatenate([key_chunk <= qry_chunk] * MLA_HEADS, axis=1)

    def block(j, diagonal):
        k = k_ref[pl.ds(pl.multiple_of(j * tq, tq), tq), :]
        v_t = k[:, :KV_LORA].astype(F32).T.astype(BF16)
        s = _dot_nt(k, q) * MLA_SCALE
        if diagonal:
            s = jnp.where(allowed, s, -jnp.inf)
        m_old = m_ref[...]
        m_new = jnp.maximum(m_old, jnp.max(s, axis=0, keepdims=True))
        alpha = jnp.exp(m_old - m_new)
        p = jnp.exp(s - m_new)
        l_ref[...] = alpha * l_ref[...] + jnp.sum(p, axis=0, keepdims=True)
        acc_ref[...] = alpha * acc_ref[...] + _dot(v_t, p.astype(BF16))
        m_ref[...] = m_new

    def body(j, carry):
        block(j, False)
        return carry

    lax.fori_loop(0, i, body, 0)
    block(i, True)
    o = (acc_ref[...] / l_ref[...]).astype(BF16)
    ys = [_dot(wuvt_ref[h], o[:, h * tq:(h + 1) * tq]) for h in range(MLA_HEADS)]
    y = jnp.concatenate(ys, axis=0)
    y = y * lax.rsqrt(jnp.mean(y * y, axis=0, keepdims=True) + EPS) * gob_ref[...]
    yb_ref[...] = y.T.astype(BF16)


def _mla_attn(qcat, kcat, w_uvtt, g_out_b_col, batch, seq, tq):
    n = batch * seq
    nq = seq // tq
    return pl.pallas_call(
        functools.partial(_mla_attn_kernel, tq=tq),
        grid=(batch, nq),
        in_specs=[pl.BlockSpec((MLA_HEADS, tq, QCAT), lambda b, i: (0, b * nq + i, 0)),
                  pl.BlockSpec((seq, QCAT), lambda b, i: (b, 0)),
                  _full(w_uvtt.shape), _full((MLA_DIM, 1))],
        out_specs=pl.BlockSpec((tq, MLA_DIM), lambda b, i: (b * nq + i, 0)),
        out_shape=jax.ShapeDtypeStruct((n, MLA_DIM), BF16),
        scratch_shapes=[pltpu.VMEM((1, MLA_HEADS * tq), F32), pltpu.VMEM((1, MLA_HEADS * tq), F32),
                        pltpu.VMEM((KV_LORA, MLA_HEADS * tq), F32)],
        compiler_params=_cparams(2), name="mla_attn",
    )(qcat, kcat, w_uvtt, g_out_b_col)


def _mla_dec_kernel(q_ref, cc_ref, ck_ref, kn_ref, wuv_ref, gob_ref, yb_ref, *, t):
    rows = MLA_HEADS * t
    q = q_ref[...].reshape(rows, QCAT)
    cc = cc_ref[0].astype(BF16)
    ck = ck_ref[0].astype(BF16)
    kn = kn_ref[...]
    s_c = (_dot_nt(q[:, :KV_LORA], cc) + _dot_nt(q[:, KV_LORA:], ck)) * MLA_SCALE
    s_n = _dot_nt(q, kn) * MLA_SCALE
    m = jnp.maximum(jnp.max(s_c, axis=-1, keepdims=True), jnp.max(s_n, axis=-1, keepdims=True))
    p_c = jnp.exp(s_c - m)
    p_n = jnp.exp(s_n - m)
    l = jnp.sum(p_c, axis=-1, keepdims=True) + jnp.sum(p_n, axis=-1, keepdims=True)
    o = (_dot(p_c.astype(BF16), cc) + _dot(p_n.astype(BF16), kn[:, :KV_LORA])) / l
    yb_ref[...] = _mla_finish(o, wuv_ref, gob_ref, t)


def _mla_dec(qcat, cache_ckv, cache_krope, kcat, w_uvt, g_out_b, batch, t):
    past = cache_ckv.shape[1]
    return pl.pallas_call(
        functools.partial(_mla_dec_kernel, t=t),
        grid=(batch,),
        in_specs=[pl.BlockSpec((MLA_HEADS, t, QCAT), lambda b: (0, b, 0)),
                  pl.BlockSpec((1, past, KV_LORA), lambda b: (b, 0, 0)),
                  pl.BlockSpec((1, past, QK_ROPE), lambda b: (b, 0, 0)),
                  pl.BlockSpec((t, QCAT), lambda b: (b, 0)),
                  _full(w_uvt.shape), _full((1, MLA_DIM))],
        out_specs=pl.BlockSpec((t, MLA_DIM), lambda b: (b, 0)),
        out_shape=jax.ShapeDtypeStruct((batch * t, MLA_DIM), BF16),
        compiler_params=_cparams(1), name="mla_dec",
    )(qcat, cache_ckv, cache_krope, kcat, w_uvt, g_out_b)


def _memkv_kernel(mem_ref, g_ref, wk_ref, wv_ref, mk_ref, mv_ref):
    mn = _rms(mem_ref[...], g_ref[...]).astype(BF16)
    mk_ref[...] = _dot(mn, wk_ref[...])
    mv_ref[...] = _dot(mn, wv_ref[...])


def _memkv(mem, g_mem, w_mk, w_mv):
    n = mem.shape[0]
    tm = MEM_TOKENS
    return pl.pallas_call(
        _memkv_kernel,
        grid=(n // tm,),
        in_specs=[pl.BlockSpec((tm, D_MODEL), lambda i: (i, 0)), _full((1, D_MODEL)),
                  _full(w_mk.shape), _full(w_mv.shape)],
        out_specs=[pl.BlockSpec((tm, MEM_DIM), lambda i: (i, 0))] * 2,
        out_shape=[jax.ShapeDtypeStruct((n, MEM_DIM), F32)] * 2,
        compiler_params=_cparams(1), name="mem_kv",
    )(mem, g_mem, w_mk, w_mv)


def _memattn_kernel(qm_ref, mk_ref, mv_ref, gom_ref, ym_ref):
    qm = qm_ref[...]
    outs = []
    for h in range(MEM_HEADS):
        sl = slice(h * MEM_HEAD_DIM, (h + 1) * MEM_HEAD_DIM)
        s = _dot_nt(qm[:, sl].astype(BF16), mk_ref[:, sl].astype(BF16)) * MEM_SCALE
        e = jnp.exp(s - jnp.max(s, axis=-1, keepdims=True))
        p = e / jnp.sum(e, axis=-1, keepdims=True)
        outs.append(_dot(p.astype(BF16), mv_ref[:, sl].astype(BF16)))
    ym_ref[...] = _rms(jnp.concatenate(outs, axis=-1), gom_ref[...]).astype(BF16)


def _memattn(q_m, mk, mv, g_out_m, tm, tiles_per_batch):
    n = q_m.shape[0]
    kv = pl.BlockSpec((MEM_TOKENS, MEM_DIM), lambda i: (i // tiles_per_batch, 0))
    return pl.pallas_call(
        _memattn_kernel,
        grid=(n // tm,),
        in_specs=[pl.BlockSpec((tm, MEM_DIM), lambda i: (i, 0)), kv, kv, _full((1, MEM_DIM))],
        out_specs=pl.BlockSpec((tm, MEM_DIM), lambda i: (i, 0)),
        out_shape=jax.ShapeDtypeStruct((n, MEM_DIM), BF16),
        compiler_params=_cparams(1), name="mem_attn",
    )(q_m, mk, mv, g_out_m)


def _merge_kernel(x_ref, ya_ref, yb_ref, ym_ref, w_ref, gf_ref, x1_ref, xf_ref):
    a0, a1 = GMLP_DIM, GMLP_DIM + MLA_DIM
    y = (_dot(ya_ref[...], w_ref[:a0, :]) + _dot(yb_ref[...], w_ref[a0:a1, :])
         + _dot(ym_ref[...], w_ref[a1:, :]))
    x1 = x_ref[...] + y
    x1_ref[...] = x1
    xf_ref[...] = _rms(x1, gf_ref[...])


def _merge(x, ya, yb, ym, w_out, g_ffn, tm):
    n = x.shape[0]
    row = lambda w: pl.BlockSpec((tm, w), lambda i: (i, 0))
    return pl.pallas_call(
        _merge_kernel,
        grid=(n // tm,),
        in_specs=[row(D_MODEL), row(GMLP_DIM), row(MLA_DIM), row(MEM_DIM), _full(w_out.shape),
                  _full((1, D_MODEL))],
        out_specs=[row(D_MODEL), row(D_MODEL)],
        out_shape=[jax.ShapeDtypeStruct((n, D_MODEL), F32)] * 2,
        compiler_params=_cparams(1), name="merge",
    )(x, ya, yb, ym, w_out, g_ffn)


def _top16(s, order, payload):
    vals, picks = [], []
    for _ in range(PEER_TOPK):
        m = jnp.max(s, axis=0, keepdims=True)
        first = jnp.min(jnp.where(s == m, order, BIG_ORDER), axis=0, keepdims=True)
        sel = order == first
        vals.append(m)
        picks.append(first if payload is None else jnp.max(jnp.where(sel, payload, -1.0), axis=0, keepdims=True))
        s = jnp.where(sel, -jnp.inf, s)
    return jnp.concatenate(vals, axis=0), jnp.concatenate(picks, axis=0)


def _pair_candidates(v1, i1, v2, i2):
    tm = v1.shape[1]
    b_iota = lax.broadcasted_iota(I32, (PEER_TOPK, tm), 0).astype(F32)
    cand, eid, order = [], [], []
    for a in range(PAIR_A_FULL):
        cand.append(v1[a:a + 1] + v2)
        eid.append(i1[a:a + 1] * N_KEYS + i2)
        order.append(b_iota + a * PEER_TOPK)
    low_a = b_iota < PAIR_A_FULL
    for b in range(PAIR_B_COLS):
        cand.append(jnp.where(low_a, -jnp.inf, v1 + v2[b:b + 1]))
        eid.append(i1 * N_KEYS + i2[b:b + 1])
        order.append(jnp.where(low_a, BIG_ORDER, b_iota * PEER_TOPK + b))
    return jnp.concatenate(cand, axis=0), jnp.concatenate(order, axis=0), jnp.concatenate(eid, axis=0)


def _peer_topk_kernel(xf_ref, wpq_ref, sk1_ref, sk2_ref, eid_ref, gate_ref, *, tm):
    q = _dot(xf_ref[...].astype(BF16), wpq_ref[...])
    key_order = lax.broadcasted_iota(I32, (N_KEYS, tm), 0).astype(F32)
    for h in range(PEER_HEADS):
        qa = q[:, h * PEER_QDIM:h * PEER_QDIM + PEER_HALF].astype(BF16)
        qb = q[:, h * PEER_QDIM + PEER_HALF:(h + 1) * PEER_QDIM].astype(BF16)
        v1, i1 = _top16(_dot_nt(sk1_ref[...], qa), key_order, None)
        v2, i2 = _top16(_dot_nt(sk2_ref[...], qb), key_order, None)
        vals, eid = _top16(*_pair_candidates(v1, i1, v2, i2))
        e = jnp.exp(vals - vals[0:1])
        rows = slice(h * PEER_TOPK, (h + 1) * PEER_TOPK)
        gate_ref[rows, :] = e / jnp.sum(e, axis=0, keepdims=True)
        eid_ref[rows, :] = eid.astype(I32)


def _peer_topk(xf, w_pq, sk1, sk2, tm):
    n = xf.shape[0]
    out = pl.BlockSpec((PEER_PAIRS, tm), lambda i: (0, i))
    return pl.pallas_call(
        functools.partial(_peer_topk_kernel, tm=tm),
        grid=(n // tm,),
        in_specs=[pl.BlockSpec((tm, D_MODEL), lambda i: (i, 0)), _full(w_pq.shape),
                  _full(sk1.shape), _full(sk2.shape)],
        out_specs=[out, out],
        out_shape=[jax.ShapeDtypeStruct((PEER_PAIRS, n), I32), jax.ShapeDtypeStruct((PEER_PAIRS, n), F32)],
        compiler_params=_cparams(1), name="peer_topk",
    )(xf, w_pq, sk1, sk2)


def _peer_pack_kernel(pu_ref, pv_ref, tab_ref, *, tr):
    for src, base in ((pu_ref, 0), (pv_ref, ROW_CHUNKS // 2)):
        x = src[...]
        words = pltpu.bitcast(pltpu.pack_elementwise([x[:, :HALF_D], x[:, HALF_D:]], packed_dtype=BF16), WORD)
        for c in range(ROW_CHUNKS // 2):
            tab_ref[pl.ds(base + c, tr, stride=ROW_CHUNKS), :] = words[:, c * LANES:(c + 1) * LANES]


def _peer_pack(peer_u, peer_v):
    ne = peer_u.shape[0]
    tr = 256
    return pl.pallas_call(
        functools.partial(_peer_pack_kernel, tr=tr),
        grid=(ne // tr,),
        in_specs=[pl.BlockSpec((tr, D_MODEL), lambda i: (i, 0))] * 2,
        out_specs=pl.BlockSpec((tr * ROW_CHUNKS, LANES), lambda i: (i, 0)),
        out_shape=jax.ShapeDtypeStruct((ne * ROW_CHUNKS, LANES), WORD),
        compiler_params=_cparams(1), name="peer_pack",
    )(peer_u, peer_v)


def _unpack(words):
    return tuple(pltpu.unpack_elementwise(words, index=i, packed_dtype=BF16, unpacked_dtype=F32) for i in (0, 1))


def _peer_mix_kernel(ids_ref, gate_ref, xf_ref, x1_ref, gfin_ref, tab_ref, y_ref, *scratch, tt):
    bufs, sems = scratch[:MIX_SLOTS], scratch[MIX_SLOTS]
    half_chunks = ROW_CHUNKS // 2
    step, last = pl.program_id(0), pl.num_programs(0) - 1

    def issue(t, k):
        for p in range(PEER_PAIRS):
            row0 = pl.multiple_of(ids_ref[0, t, p] * ROW_CHUNKS, ROW_CHUNKS)
            pltpu.make_async_copy(tab_ref.at[pl.ds(row0, ROW_CHUNKS)],
                                  bufs[k].at[pl.ds(p * ROW_PITCH, ROW_CHUNKS)],
                                  sems.at[k]).start(priority=p % 2)

    def wait(k):
        n = PEER_PAIRS * ROW_CHUNKS
        pltpu.make_async_copy(tab_ref.at[pl.ds(0, n)], bufs[k].at[pl.ds(0, n)], sems.at[k]).wait()

    def words(k, c):
        return bufs[k][pl.ds(c, PEER_PAIRS, stride=ROW_PITCH), :]

    pair_of_lane = lax.broadcasted_iota(I32, (PEER_PAIRS, 2 * PEER_PAIRS), 1) >> 1
    own_lane = pair_of_lane == lax.broadcasted_iota(I32, (PEER_PAIRS, 2 * PEER_PAIRS), 0)
    sub = lax.broadcasted_iota(I32, (8, 2 * PEER_PAIRS), 0)
    lane = lax.broadcasted_iota(I32, (8, 2 * PEER_PAIRS), 1)
    keep = (sub < 4) & ((lane & 1) == (sub >> 1))
    high_part = (sub & 1) == 0

    def gate_weights(k, xrow, grow):
        acc = jnp.zeros((PEER_PAIRS, LANES), F32)
        for c in range(half_chunks):
            lo, hi = _unpack(words(k, c))
            acc = acc + lo * xrow[:, c * LANES:(c + 1) * LANES]
            acc = acc + hi * xrow[:, HALF_D + c * LANES:HALF_D + (c + 1) * LANES]
        h = jnp.sum(acc, axis=-1, keepdims=True)
        hrow = jnp.sum(jnp.where(own_lane, h, 0.0), axis=0, keepdims=True)
        w = grow * _gelu(hrow)
        w_hi = w.astype(BF16).astype(F32)
        return jnp.where(keep, jnp.where(high_part, w_hi, w - w_hi), 0.0).astype(BF16)

    def mix_values(k, lhs):
        rhs = jnp.concatenate([pltpu.bitcast(words(k, half_chunks + c), BF16) for c in range(half_chunks)], axis=-1)
        o = _dot(lhs, rhs)
        return jnp.concatenate([o[0:1] + o[1:2], o[2:3] + o[3:4]], axis=-1)

    @pl.when(step == 0)
    def _():
        for k in range(MIX_AHEAD):
            issue(k, k)

    def group(g, carry):
        rows = pl.ds(pl.multiple_of(g * MIX_SLOTS, MIX_SLOTS), MIX_SLOTS)
        x_tile, g_tile = xf_ref[rows, :], gate_ref[rows, :]
        lhs_prev, yrows = None, []
        for k in range(MIX_SLOTS):
            wait(k)
            issue(g * MIX_SLOTS + k + MIX_AHEAD, (k + MIX_AHEAD) % MIX_SLOTS)
            lhs = gate_weights(k, x_tile[k:k + 1], g_tile[k:k + 1])
            if lhs_prev is not None:
                yrows.append(mix_values(k - 1, lhs_prev))
            lhs_prev = lhs
        yrows.append(mix_values(MIX_SLOTS - 1, lhs_prev))
        y_ref[rows, :] = _rms(x1_ref[rows, :] + jnp.concatenate(yrows, axis=0), gfin_ref[...])
        return carry

    lax.fori_loop(0, tt // MIX_SLOTS, group, 0)

    @pl.when(step == last)
    def _():
        for k in range(MIX_AHEAD):
            wait(k)


def _peer_mix(eid_t, gate_t, xf, x1, g_final, table, tt):
    n = xf.shape[0]
    steps = n // tt
    eid = eid_t.T
    look = jnp.concatenate([eid[tt:], eid[-tt:]], axis=0).reshape(steps, tt, PEER_PAIRS)[:, :MIX_SLOTS]
    ids = jnp.concatenate([eid.reshape(steps, tt, PEER_PAIRS), look], axis=1)
    gate = jnp.repeat(gate_t.T, 2, axis=1)
    row = pl.BlockSpec((tt, D_MODEL), lambda i: (i, 0))
    assert MIX_SLOTS == 8 and tt % MIX_SLOTS == 0
    return pl.pallas_call(
        functools.partial(_peer_mix_kernel, tt=tt),
        grid=(steps,),
        in_specs=[pl.BlockSpec((1, tt + MIX_SLOTS, PEER_PAIRS), lambda i: (i, 0, 0), memory_space=pltpu.SMEM),
                  pl.BlockSpec((tt, 2 * PEER_PAIRS), lambda i: (i, 0)),
                  row, row, _full((1, D_MODEL)),
                  pl.BlockSpec(memory_space=pl.ANY)],
        out_specs=row,
        out_shape=jax.ShapeDtypeStruct((n, D_MODEL), F32),
        scratch_shapes=[pltpu.VMEM((PEER_PAIRS * ROW_PITCH, LANES), WORD)] * MIX_SLOTS
        + [pltpu.SemaphoreType.DMA((MIX_SLOTS,))],
        compiler_params=_cparams(1), name="peer_mix",
    )(ids, gate, xf, x1, g_final, table)


def _rope_tables(pos):
    half = QK_ROPE // 2
    inv = ROPE_THETA ** (-jnp.arange(half, dtype=F32) / half)
    ang = pos.astype(F32)[:, None] * inv[None, :]
    cos, sin = jnp.cos(ang), jnp.sin(ang)
    return jnp.concatenate([cos, cos], -1), jnp.concatenate([sin, sin], -1)


def _rot_cols(w):
    half = w.shape[-1] // 2
    return jnp.concatenate([-w[..., half:], w[..., :half]], axis=-1)


def _group(x, tokens_per_seq, pos, cache, mem_kv, wts, g_final):
    n = x.shape[0]
    nseq = n // tokens_per_seq
    tm = min(256, n)
    cos2, sin2 = _rope_tables(pos)
    reps = max(tm // tokens_per_seq, 1)
    cos2, sin2 = jnp.tile(cos2, (reps, 1)), jnp.tile(sin2, (reps, 1))
    cos8, sin8 = jnp.tile(cos2, (1, MLA_HEADS)), jnp.tile(sin2, (1, MLA_HEADS))

    uv, c_q, ckv, krope, kcat, q_m = _proj(x, wts["g_attn"], wts["w_ext"], wts["g_kv"], cos2, sin2, tm)
    rows = min(GMLP_CHUNK, tokens_per_seq)
    ya, v = _gmlp(uv, wts["g_v"], wts["b_v"], wts["w_s"], wts["b_st"], wts["g_out_a"], rows)
    qcat = _mlaq(c_q, wts["g_q"], wts["w_uqp"], wts["w_ukt"], cos8, sin8, tm)
    if cache is None:
        yb = _mla_attn(qcat, kcat, jnp.swapaxes(wts["w_uvt"], 1, 2), wts["g_out_b"].reshape(MLA_DIM, 1), nseq,
                       tokens_per_seq, min(256, tokens_per_seq))
    else:
        yb = _mla_dec(qcat, cache[0], cache[1], kcat, wts["w_uvt"], wts["g_out_b"], nseq, tokens_per_seq)
    mk, mv = mem_kv
    tma = min(tm, tokens_per_seq)
    ym = _memattn(q_m, mk, mv, wts["g_out_m"], tma, tokens_per_seq // tma)
    x1, xf = _merge(x, ya, yb, ym, wts["w_out"], wts["g_ffn"], tm)
    eid_t, gate_t = _peer_topk(xf, wts["w_pq"], wts["sk1"], wts["sk2"], tm)
    y = _peer_mix(eid_t, gate_t, xf, x1, g_final, wts["table"], min(64, n))
    return y, ckv, krope, v


def kernel(x_prompt, x_sample, cache_mla_ckv, cache_mla_krope, cache_mem_k, cache_mem_v, mem_prompt, g_attn, w_in, g_v, b_v, w_s, b_s, g_q, w_uq, w_uk, w_uv, g_kv, g_mem, w_mk, w_mv, g_out_a, g_out_b, g_out_m, w_out, g_ffn, w_pq, sub_keys1, sub_keys2, peer_u, peer_v, g_final):
    assert w_in.shape[0] == 1, "the final norm is fused after the single layer"
    l = 0
    bp, sp, _ = x_prompt.shape
    bs, ts, _ = x_sample.shape
    past = cache_mla_ckv.shape[2]
    gfin = g_final.reshape(1, D_MODEL)
    wi = w_in[l]
    w_ext = jnp.concatenate([wi[:, :_C_KROT], _rot_cols(wi[:, _C_KR:_C_KROT]), wi[:, _C_KROT:]], axis=1)
    wq = w_uq[l].reshape(Q_LORA, MLA_HEADS, QK_NOPE + QK_ROPE)
    wq_rope = wq[:, :, QK_NOPE:]
    w_uqp = jnp.concatenate([wq[:, :, :QK_NOPE].reshape(Q_LORA, -1), wq_rope.reshape(Q_LORA, -1),
                             _rot_cols(wq_rope).reshape(Q_LORA, -1)], axis=1)
    wts = {
        "g_attn": g_attn[l].reshape(1, -1), "w_ext": w_ext.astype(BF16), "g_kv": g_kv[l].reshape(1, -1),
        "g_v": g_v[l].reshape(1, -1), "b_v": b_v[l].reshape(1, -1), "w_s": w_s[l], "b_st": b_s[l].T,
        "g_out_a": g_out_a[l].reshape(1, -1), "g_q": g_q[l].reshape(1, -1), "w_uqp": w_uqp.astype(BF16),
        "w_ukt": jnp.transpose(w_uk[l], (1, 2, 0)).astype(BF16),
        "w_uvt": jnp.transpose(w_uv[l], (1, 0, 2)).astype(BF16),
        "g_out_b": g_out_b[l].reshape(1, -1), "g_out_m": g_out_m[l].reshape(1, -1),
        "w_out": w_out[l].astype(BF16), "g_ffn": g_ffn[l].reshape(1, -1), "w_pq": w_pq[l].astype(BF16),
        "sk1": sub_keys1[l].astype(BF16), "sk2": sub_keys2[l].astype(BF16),
        "table": _peer_pack(peer_u[l], peer_v[l]),
    }
    mk, mv = _memkv(mem_prompt.reshape(bp * MEM_TOKENS, D_MODEL), g_mem[l].reshape(1, -1),
                    w_mk[l].astype(BF16), w_mv[l].astype(BF16))
    yp, ckv_p, kr_p, _ = _group(x_prompt.reshape(bp * sp, D_MODEL), sp, jnp.arange(sp), None, (mk, mv), wts, gfin)
    mem_s = (cache_mem_k[l].reshape(bs * MEM_TOKENS, MEM_DIM), cache_mem_v[l].reshape(bs * MEM_TOKENS, MEM_DIM))
    ys, ckv_s, kr_s, gv_s = _group(x_sample.reshape(bs * ts, D_MODEL), ts, past + jnp.arange(ts),
                                   (cache_mla_ckv[l], cache_mla_krope[l]), mem_s, wts, gfin)
    return (yp.reshape(bp, sp, D_MODEL), ys.reshape(bs, ts, D_MODEL),
            ckv_p.reshape(1, bp, sp, KV_LORA), kr_p.reshape(1, bp, sp, QK_ROPE),
            mk.reshape(1, bp, MEM_TOKENS, MEM_HEADS, MEM_HEAD_DIM),
            mv.reshape(1, bp, MEM_TOKENS, MEM_HEADS, MEM_HEAD_DIM),
            ckv_s.reshape(1, bs, ts, KV_LORA), kr_s.reshape(1, bs, ts, QK_ROPE),
            gv_s.reshape(1, bs, ts, GMLP_DIM))
```

```python
import functools
import math

import jax
import jax.numpy as jnp
from jax import lax
from jax.experimental import pallas as pl
from jax.experimental.pallas import tpu as pltpu

F32 = jnp.float32
BF16 = jnp.bfloat16
I32 = jnp.int32
WORD = jnp.uint32

D_MODEL = 2048
CHUNK = 64
CHUNK_SHIFT = 6
EPS = 1e-6
GMLP_CHUNK = 128
GMLP_DIM = 512
GMLP_GROUPS = 4
GMLP_GROUP_DIM = 128
V_HEAD = 128
QK_NOPE = 128
QK_ROPE = 64
MLA_HEADS = 8
MLA_DIM = 1024
Q_LORA = 512
KV_LORA = 256
ROPE_THETA = 10000.0
MLA_SCALE = (QK_NOPE + QK_ROPE) ** -0.5
QCAT = KV_LORA + QK_ROPE
MEM_TOKENS = 256
MEM_HEADS = 4
MEM_DIM = 512
MEM_HEAD_DIM = 128
MEM_SCALE = MEM_HEAD_DIM ** -0.5
PEER_HEADS = 8
N_KEYS = 128
PEER_QDIM = 256
PEER_HALF = 128
PEER_TOPK = 16
PEER_PAIRS = PEER_HEADS * PEER_TOPK
PAIR_ROWS_LISTED = 5
BIG_ORDER = 1e9
HALF_D = D_MODEL // 2
LANES = 128
ROW_CHUNKS = 2 * HALF_D // LANES
ROW_PITCH = 17
MIX_SLOTS = 8
MIX_AHEAD = 6

_C_UV, _C_Q, _C_KV, _C_KR, _C_KROT, _C_M, _C_END = 0, 1024, 1536, 1792, 1856, 1920, 2432

VMEM_LIMIT = 48 * 1024 * 1024


def _cparams(n_grid):
    return pltpu.CompilerParams(dimension_semantics=("arbitrary",) * n_grid,
                                vmem_limit_bytes=VMEM_LIMIT)


def _rms(x, g):
    return x * lax.rsqrt(jnp.mean(x * x, axis=-1, keepdims=True) + EPS) * g


def _gelu(x):
    return x * (0.5 * (1.0 + jnp.tanh(math.sqrt(2.0 / math.pi) * (x + 0.044715 * (x * x * x)))))


def _dot(a, b):
    return jnp.dot(a, b, preferred_element_type=F32)


def _dot_nt(a, b):
    return lax.dot_general(a, b, (((1,), (1,)), ((), ())), preferred_element_type=F32)


def _full(shape):
    n = len(shape)
    return pl.BlockSpec(shape, lambda *_: (0,) * n)


def _proj_kernel(x_ref, g_ref, w_ref, gkv_ref, cos_ref, sin_ref,
                 uv_ref, cq_ref, ckv_ref, kr_ref, kcat_ref, qm_ref):
    xn = _rms(x_ref[...], g_ref[...])
    proj = _dot(xn.astype(BF16), w_ref[...])
    uv_ref[...] = proj[:, _C_UV:_C_Q]
    cq_ref[...] = proj[:, _C_Q:_C_KV]
    ckv = _rms(proj[:, _C_KV:_C_KR], gkv_ref[...])
    ckv_ref[...] = ckv
    krope = proj[:, _C_KR:_C_KROT] * cos_ref[...] + proj[:, _C_KROT:_C_M] * sin_ref[...]
    kr_ref[...] = krope
    kcat_ref[:, :KV_LORA] = ckv.astype(BF16)
    kcat_ref[:, KV_LORA:] = krope.astype(BF16)
    qm_ref[...] = proj[:, _C_M:_C_END]


def _proj(x, g_attn, w_ext, g_kv, cos2, sin2, tm):
    n = x.shape[0]
    nper = cos2.shape[0] // tm
    row = lambda w: pl.BlockSpec((tm, w), lambda i: (i, 0))
    tab = pl.BlockSpec((tm, QK_ROPE), lambda i: (i % nper, 0))
    return pl.pallas_call(
        _proj_kernel,
        grid=(n // tm,),
        in_specs=[row(D_MODEL), _full((1, D_MODEL)), _full(w_ext.shape), _full((1, KV_LORA)), tab, tab],
        out_specs=[row(2 * GMLP_DIM), row(Q_LORA), row(KV_LORA), row(QK_ROPE), row(QCAT), row(MEM_DIM)],
        out_shape=[jax.ShapeDtypeStruct((n, 2 * GMLP_DIM), F32), jax.ShapeDtypeStruct((n, Q_LORA), F32),
                   jax.ShapeDtypeStruct((n, KV_LORA), F32), jax.ShapeDtypeStruct((n, QK_ROPE), F32),
                   jax.ShapeDtypeStruct((n, QCAT), BF16), jax.ShapeDtypeStruct((n, MEM_DIM), F32)],
        compiler_params=_cparams(1), name="proj",
    )(x, g_attn, w_ext, g_kv, cos2, sin2)


def _gmlp_kernel(uv_ref, gv_ref, bv_ref, ws_ref, bst_ref, goa_ref, ya_ref, v_ref, *, rows):
    uv = _gelu(uv_ref[...])
    u = uv[:, :GMLP_DIM]
    vr = uv[:, GMLP_DIM:]
    mu = jnp.mean(vr, axis=-1, keepdims=True)
    var = jnp.mean(jnp.square(vr - mu), axis=-1, keepdims=True)
    v = (vr - mu) * lax.rsqrt(var + EPS) * gv_ref[...] + bv_ref[...]
    v_ref[...] = v
    r = lax.broadcasted_iota(I32, (rows, rows), 0)
    c = lax.broadcasted_iota(I32, (rows, rows), 1)
    zs = []
    for g in range(GMLP_GROUPS):
        w = jnp.where(r >= c, ws_ref[g, :rows, :rows], 0.0).astype(BF16)
        vg = v[:, g * GMLP_GROUP_DIM:(g + 1) * GMLP_GROUP_DIM].astype(BF16)
        zs.append(_dot(w, vg) + bst_ref[:rows, g:g + 1])
    y = u * jnp.concatenate(zs, axis=-1)
    ya_ref[...] = _rms(y, goa_ref[...]).astype(BF16)


def _gmlp(uv, g_v, b_v, w_s, b_st, g_out_a, rows):
    n = uv.shape[0]
    row = lambda w: pl.BlockSpec((rows, w), lambda i: (i, 0))
    return pl.pallas_call(
        functools.partial(_gmlp_kernel, rows=rows),
        grid=(n // rows,),
        in_specs=[row(2 * GMLP_DIM), _full((1, GMLP_DIM)), _full((1, GMLP_DIM)), _full(w_s.shape),
                  _full(b_st.shape), _full((1, GMLP_DIM))],
        out_specs=[row(GMLP_DIM), row(GMLP_DIM)],
        out_shape=[jax.ShapeDtypeStruct((n, GMLP_DIM), BF16), jax.ShapeDtypeStruct((n, GMLP_DIM), F32)],
        compiler_params=_cparams(1), name="gmlp",
    )(uv, g_v, b_v, w_s, b_st, g_out_a)


def _mlaq_kernel(cq_ref, gq_ref, wq_ref, wuk_ref, cos_ref, sin_ref, q_ref):
    cq = _rms(cq_ref[...], gq_ref[...])
    q = _dot(cq.astype(BF16), wq_ref[...])
    nr = MLA_HEADS * QK_NOPE
    rw = MLA_HEADS * QK_ROPE
    qrope = q[:, nr:nr + rw] * cos_ref[...] + q[:, nr + rw:] * sin_ref[...]
    for h in range(MLA_HEADS):
        qn = q[:, h * QK_NOPE:(h + 1) * QK_NOPE].astype(BF16)
        q_ref[h, :, :KV_LORA] = _dot(qn, wuk_ref[h]).astype(BF16)
        q_ref[h, :, KV_LORA:] = qrope[:, h * QK_ROPE:(h + 1) * QK_ROPE].astype(BF16)


def _mlaq(c_q, g_q, w_uqp, w_ukt, cos8, sin8, tm):
    n = c_q.shape[0]
    nper = cos8.shape[0] // tm
    tab = pl.BlockSpec((tm, MLA_HEADS * QK_ROPE), lambda i: (i % nper, 0))
    return pl.pallas_call(
        _mlaq_kernel,
        grid=(n // tm,),
        in_specs=[pl.BlockSpec((tm, Q_LORA), lambda i: (i, 0)), _full((1, Q_LORA)), _full(w_uqp.shape),
                  _full(w_ukt.shape), tab, tab],
        out_specs=pl.BlockSpec((MLA_HEADS, tm, QCAT), lambda i: (0, i, 0)),
        out_shape=jax.ShapeDtypeStruct((MLA_HEADS, n, QCAT), BF16),
        compiler_params=_cparams(1), name="mla_q",
    )(c_q, g_q, w_uqp, w_ukt, cos8, sin8)


def _mla_finish(o, wuv_ref, gob_ref, tq):
    ys = [_dot(o[h * tq:(h + 1) * tq].astype(BF16), wuv_ref[h]) for h in range(MLA_HEADS)]
    return _rms(jnp.concatenate(ys, axis=-1), gob_ref[...]).astype(BF16)


def _mla_attn_kernel(q_ref, k_ref, wuvt_ref, gob_ref, yb_ref, m_ref, l_ref, acc_ref, *, tq):
    i = pl.program_id(1)
    cols = MLA_HEADS * tq
    q = q_ref[...].reshape(cols, QCAT)
    m_ref[...] = jnp.full((1, cols), -jnp.inf, F32)
    l_ref[...] = jnp.zeros((1, cols), F32)
    acc_ref[...] = jnp.zeros((KV_LORA, cols), F32)
    key_chunk = lax.broadcasted_iota(I32, (tq, tq), 0) >> CHUNK_SHIFT
    qry_chunk = lax.broadcasted_iota(I32, (tq, tq), 1) >> CHUNK_SHIFT
    allowed = jnp.concatenate([key_chunk <= qry_chunk] * MLA_HEADS, axis=1)

    def block(j, diagonal):
        k = k_ref[pl.ds(pl.multiple_of(j * tq, tq), tq), :]
        v_t = k[:, :KV_LORA].astype(F32).T.astype(BF16)
        s = _dot_nt(k, q) * MLA_SCALE
        if diagonal:
            s = jnp.where(allowed, s, -jnp.inf)
        m_old = m_ref[...]
        m_new = jnp.maximum(m_old, jnp.max(s, axis=0, keepdims=True))
        alpha = jnp.exp(m_old - m_new)
        p = jnp.exp(s - m_new)
        l_ref[...] = alpha * l_ref[...] + jnp.sum(p, axis=0, keepdims=True)
        acc_ref[...] = alpha * acc_ref[...] + _dot(v_t, p.astype(BF16))
        m_ref[...] = m_new

    def body(j, carry):
        block(j, False)
        return carry

    lax.fori_loop(0, i, body, 0)
    block(i, True)
    o = (acc_ref[...] / l_ref[...]).astype(BF16)
    ys = [_dot(wuvt_ref[h], o[:, h * tq:(h + 1) * tq]) for h in range(MLA_HEADS)]
    y = jnp.concatenate(ys, axis=0)
    y = y * lax.rsqrt(jnp.mean(y * y, axis=0, keepdims=True) + EPS) * gob_ref[...]
    yb_ref[...] = y.T.astype(BF16)


def _mla_attn(qcat, kcat, w_uvtt, g_out_b_col, batch, seq, tq):
    n = batch * seq
    nq = seq // tq
    return pl.pallas_call(
        functools.partial(_mla_attn_kernel, tq=tq),
        grid=(batch, nq),
        in_specs=[pl.BlockSpec((MLA_HEADS, tq, QCAT), lambda b, i: (0, b * nq + i, 0)),
                  pl.BlockSpec((seq, QCAT), lambda b, i: (b, 0)),
                  _full(w_uvtt.shape), _full((MLA_DIM, 1))],
        out_specs=pl.BlockSpec((tq, MLA_DIM), lambda b, i: (b * nq + i, 0)),
        out_shape=jax.ShapeDtypeStruct((n, MLA_DIM), BF16),
        scratch_shapes=[pltpu.VMEM((1, MLA_HEADS * tq), F32), pltpu.VMEM((1, MLA_HEADS * tq), F32),
                        pltpu.VMEM((KV_LORA, MLA_HEADS * tq), F32)],
        compiler_params=_cparams(2), name="mla_attn",
    )(qcat, kcat, w_uvtt, g_out_b_col)


def _mla_dec_kernel(q_ref, cc_ref, ck_ref, kn_ref, wuv_ref, gob_ref, yb_ref, *, t):
    rows = MLA_HEADS * t
    q = q_ref[...].reshape(rows, QCAT)
    cc = cc_ref[0].astype(BF16)
    ck = ck_ref[0].astype(BF16)
    kn = kn_ref[...]
    s_c = (_dot_nt(q[:, :KV_LORA], cc) + _dot_nt(q[:, KV_LORA:], ck)) * MLA_SCALE
    s_n = _dot_nt(q, kn) * MLA_SCALE
    m = jnp.maximum(jnp.max(s_c, axis=-1, keepdims=True), jnp.max(s_n, axis=-1, keepdims=True))
    p_c = jnp.exp(s_c - m)
    p_n = jnp.exp(s_n - m)
    l = jnp.sum(p_c, axis=-1, keepdims=True) + jnp.sum(p_n, axis=-1, keepdims=True)
    o = (_dot(p_c.astype(BF16), cc) + _dot(p_n.astype(BF16), kn[:, :KV_LORA])) / l
    yb_ref[...] = _mla_finish(o, wuv_ref, gob_ref, t)


def _mla_dec(qcat, cache_ckv, cache_krope, kcat, w_uvt, g_out_b, batch, t):
    past = cache_ckv.shape[1]
    return pl.pallas_call(
        functools.partial(_mla_dec_kernel, t=t),
        grid=(batch,),
        in_specs=[pl.BlockSpec((MLA_HEADS, t, QCAT), lambda b: (0, b, 0)),
                  pl.BlockSpec((1, past, KV_LORA), lambda b: (b, 0, 0)),
                  pl.BlockSpec((1, past, QK_ROPE), lambda b: (b, 0, 0)),
                  pl.BlockSpec((t, QCAT), lambda b: (b, 0)),
                  _full(w_uvt.shape), _full((1, MLA_DIM))],
        out_specs=pl.BlockSpec((t, MLA_DIM), lambda b: (b, 0)),
        out_shape=jax.ShapeDtypeStruct((batch * t, MLA_DIM), BF16),
        compiler_params=_cparams(1), name="mla_dec",
    )(qcat, cache_ckv, cache_krope, kcat, w_uvt, g_out_b)


def _memkv_kernel(mem_ref, g_ref, wk_ref, wv_ref, mk_ref, mv_ref):
    mn = _rms(mem_ref[...], g_ref[...]).astype(BF16)
    mk_ref[...] = _dot(mn, wk_ref[...])
    mv_ref[...] = _dot(mn, wv_ref[...])


def _memkv(mem, g_mem, w_mk, w_mv):
    n = mem.shape[0]
    tm = MEM_TOKENS
    return pl.pallas_call(
        _memkv_kernel,
        grid=(n // tm,),
        in_specs=[pl.BlockSpec((tm, D_MODEL), lambda i: (i, 0)), _full((1, D_MODEL)),
                  _full(w_mk.shape), _full(w_mv.shape)],
        out_specs=[pl.BlockSpec((tm, MEM_DIM), lambda i: (i, 0))] * 2,
        out_shape=[jax.ShapeDtypeStruct((n, MEM_DIM), F32)] * 2,
        compiler_params=_cparams(1), name="mem_kv",
    )(mem, g_mem, w_mk, w_mv)


def _memattn_kernel(qm_ref, mk_ref, mv_ref, gom_ref, ym_ref):
    qm = qm_ref[...]
    outs = []
    for h in range(MEM_HEADS):
        sl = slice(h * MEM_HEAD_DIM, (h + 1) * MEM_HEAD_DIM)
        s = _dot_nt(qm[:, sl].astype(BF16), mk_ref[:, sl].astype(BF16)) * MEM_SCALE
        e = jnp.exp(s - jnp.max(s, axis=-1, keepdims=True))
        p = e / jnp.sum(e, axis=-1, keepdims=True)
        outs.append(_dot(p.astype(BF16), mv_ref[:, sl].astype(BF16)))
    ym_ref[...] = _rms(jnp.concatenate(outs, axis=-1), gom_ref[...]).astype(BF16)


def _memattn(q_m, mk, mv, g_out_m, tm, tiles_per_batch):
    n = q_m.shape[0]
    kv = pl.BlockSpec((MEM_TOKENS, MEM_DIM), lambda i: (i // tiles_per_batch, 0))
    return pl.pallas_call(
        _memattn_kernel,
        grid=(n // tm,),
        in_specs=[pl.BlockSpec((tm, MEM_DIM), lambda i: (i, 0)), kv, kv, _full((1, MEM_DIM))],
        out_specs=pl.BlockSpec((tm, MEM_DIM), lambda i: (i, 0)),
        out_shape=jax.ShapeDtypeStruct((n, MEM_DIM), BF16),
        compiler_params=_cparams(1), name="mem_attn",
    )(q_m, mk, mv, g_out_m)


def _merge_kernel(x_ref, ya_ref, yb_ref, ym_ref, w_ref, gf_ref, x1_ref, xf_ref):
    a0, a1 = GMLP_DIM, GMLP_DIM + MLA_DIM
    y = (_dot(ya_ref[...], w_ref[:a0, :]) + _dot(yb_ref[...], w_ref[a0:a1, :])
         + _dot(ym_ref[...], w_ref[a1:, :]))
    x1 = x_ref[...] + y
    x1_ref[...] = x1
    xf_ref[...] = _rms(x1, gf_ref[...])


def _merge(x, ya, yb, ym, w_out, g_ffn, tm):
    n = x.shape[0]
    row = lambda w: pl.BlockSpec((tm, w), lambda i: (i, 0))
    return pl.pallas_call(
        _merge_kernel,
        grid=(n // tm,),
        in_specs=[row(D_MODEL), row(GMLP_DIM), row(MLA_DIM), row(MEM_DIM), _full(w_out.shape),
                  _full((1, D_MODEL))],
        out_specs=[row(D_MODEL), row(D_MODEL)],
        out_shape=[jax.ShapeDtypeStruct((n, D_MODEL), F32)] * 2,
        compiler_params=_cparams(1), name="merge",
    )(x, ya, yb, ym, w_out, g_ffn)


def _top16(s, order, payload):
    vals, picks = [], []
    for _ in range(PEER_TOPK):
        m = jnp.max(s, axis=0, keepdims=True)
        first = jnp.min(jnp.where(s == m, order, BIG_ORDER), axis=0, keepdims=True)
        sel = order == first
        vals.append(m)
        picks.append(first if payload is None else jnp.max(jnp.where(sel, payload, -1.0), axis=0, keepdims=True))
        s = jnp.where(sel, -jnp.inf, s)
    return jnp.concatenate(vals, axis=0), jnp.concatenate(picks, axis=0)


def _pair_candidates(v1, i1, v2, i2):
    tm = v1.shape[1]
    row = lax.broadcasted_iota(I32, (PEER_TOPK, tm), 0)
    top = row < 8
    low = (row & 7).astype(F32)
    rowf = row.astype(F32)
    halves = lambda x: jnp.where(top, x, pltpu.roll(x, 8, 0))
    v1h, i1h, v2h, i2h = halves(v1), halves(i1), halves(v2), halves(i2)
    cand = [v1[0:1] + v2]
    eid = [i1[0:1] * N_KEYS + i2]
    order = [rowf]
    for a0, a1 in ((1, 2), (3, 4)):
        cand.append(jnp.where(top, v1[a0:a0 + 1], v1[a1:a1 + 1]) + v2h)
        eid.append(jnp.where(top, i1[a0:a0 + 1], i1[a1:a1 + 1]) * N_KEYS + i2h)
        order.append(jnp.where(top, a0 * PEER_TOPK, a1 * PEER_TOPK) + low)
    listed = rowf < PAIR_ROWS_LISTED
    cand.append(jnp.where(listed, -jnp.inf, v1 + v2[0:1]))
    eid.append(i1 * N_KEYS + i2[0:1])
    order.append(jnp.where(listed, BIG_ORDER, rowf * PEER_TOPK))
    listed = low < PAIR_ROWS_LISTED
    cand.append(jnp.where(listed, -jnp.inf, v1h + jnp.where(top, v2[1:2], v2[2:3])))
    eid.append(i1h * N_KEYS + jnp.where(top, i2[1:2], i2[2:3]))
    order.append(jnp.where(listed, BIG_ORDER, low * PEER_TOPK + jnp.where(top, 1.0, 2.0)))
    return jnp.concatenate(cand, axis=0), jnp.concatenate(order, axis=0), jnp.concatenate(eid, axis=0)


def _peer_topk_kernel(xf_ref, wpq_ref, sk1_ref, sk2_ref, eid_ref, gate_ref, *, tm):
    q = _dot(xf_ref[...].astype(BF16), wpq_ref[...])
    key_order = lax.broadcasted_iota(I32, (N_KEYS, tm), 0).astype(F32)
    for h in range(PEER_HEADS):
        qa = q[:, h * PEER_QDIM:h * PEER_QDIM + PEER_HALF].astype(BF16)
        qb = q[:, h * PEER_QDIM + PEER_HALF:(h + 1) * PEER_QDIM].astype(BF16)
        v1, i1 = _top16(_dot_nt(sk1_ref[...], qa), key_order, None)
        v2, i2 = _top16(_dot_nt(sk2_ref[...], qb), key_order, None)
        vals, eid = _top16(*_pair_candidates(v1, i1, v2, i2))
        e = jnp.exp(vals - vals[0:1])
        rows = slice(h * PEER_TOPK, (h + 1) * PEER_TOPK)
        gate_ref[rows, :] = e / jnp.sum(e, axis=0, keepdims=True)
        eid_ref[rows, :] = eid.astype(I32)


def _peer_topk(xf, w_pq, sk1, sk2, tm):
    n = xf.shape[0]
    out = pl.BlockSpec((PEER_PAIRS, tm), lambda i: (0, i))
    return pl.pallas_call(
        functools.partial(_peer_topk_kernel, tm=tm),
        grid=(n // tm,),
        in_specs=[pl.BlockSpec((tm, D_MODEL), lambda i: (i, 0)), _full(w_pq.shape),
                  _full(sk1.shape), _full(sk2.shape)],
        out_specs=[out, out],
        out_shape=[jax.ShapeDtypeStruct((PEER_PAIRS, n), I32), jax.ShapeDtypeStruct((PEER_PAIRS, n), F32)],
        compiler_params=_cparams(1), name="peer_topk",
    )(xf, w_pq, sk1, sk2)


def _peer_pack_kernel(pu_ref, pv_ref, tab_ref, *, tr):
    for src, base in ((pu_ref, 0), (pv_ref, ROW_CHUNKS // 2)):
        x = src[...]
        words = pltpu.bitcast(pltpu.pack_elementwise([x[:, :HALF_D], x[:, HALF_D:]], packed_dtype=BF16), WORD)
        for c in range(ROW_CHUNKS // 2):
            tab_ref[pl.ds(base + c, tr, stride=ROW_CHUNKS), :] = words[:, c * LANES:(c + 1) * LANES]


def _peer_pack(peer_u, peer_v):
    ne = peer_u.shape[0]
    tr = 256
    return pl.pallas_call(
        functools.partial(_peer_pack_kernel, tr=tr),
        grid=(ne // tr,),
        in_specs=[pl.BlockSpec((tr, D_MODEL), lambda i: (i, 0))] * 2,
        out_specs=pl.BlockSpec((tr * ROW_CHUNKS, LANES), lambda i: (i, 0)),
        out_shape=jax.ShapeDtypeStruct((ne * ROW_CHUNKS, LANES), WORD),
        compiler_params=_cparams(1), name="peer_pack",
    )(peer_u, peer_v)


def _unpack(words):
    return tuple(pltpu.unpack_elementwise(words, index=i, packed_dtype=BF16, unpacked_dtype=F32) for i in (0, 1))


def _peer_mix_kernel(ids_ref, gate_ref, xf_ref, x1_ref, gfin_ref, tab_ref, y_ref, *scratch, tt):
    bufs, sems = scratch[:MIX_SLOTS], scratch[MIX_SLOTS]
    half_chunks = ROW_CHUNKS // 2
    step, last = pl.program_id(0), pl.num_programs(0) - 1

    def issue(t, k):
        for p in range(PEER_PAIRS):
            row0 = pl.multiple_of(ids_ref[0, t, p] * ROW_CHUNKS, ROW_CHUNKS)
            pltpu.make_async_copy(tab_ref.at[pl.ds(row0, ROW_CHUNKS)],
                                  bufs[k].at[pl.ds(p * ROW_PITCH, ROW_CHUNKS)],
                                  sems.at[k]).start(priority=p % 2)

    def wait(k):
        n = PEER_PAIRS * ROW_CHUNKS
        pltpu.make_async_copy(tab_ref.at[pl.ds(0, n)], bufs[k].at[pl.ds(0, n)], sems.at[k]).wait()

    def words(k, c):
        return bufs[k][pl.ds(c, PEER_PAIRS, stride=ROW_PITCH), :]

    pair_of_lane = lax.broadcasted_iota(I32, (PEER_PAIRS, 2 * PEER_PAIRS), 1) >> 1
    own_lane = pair_of_lane == lax.broadcasted_iota(I32, (PEER_PAIRS, 2 * PEER_PAIRS), 0)
    sub = lax.broadcasted_iota(I32, (8, 2 * PEER_PAIRS), 0)
    lane = lax.broadcasted_iota(I32, (8, 2 * PEER_PAIRS), 1)
    keep = (sub < 4) & ((lane & 1) == (sub >> 1))
    high_part = (sub & 1) == 0

    def gate_weights(k, xrow, grow):
        acc = jnp.zeros((PEER_PAIRS, LANES), F32)
        for c in range(half_chunks):
            lo, hi = _unpack(words(k, c))
            acc = acc + lo * xrow[:, c * LANES:(c + 1) * LANES]
            acc = acc + hi * xrow[:, HALF_D + c * LANES:HALF_D + (c + 1) * LANES]
        h = jnp.sum(acc, axis=-1, keepdims=True)
        hrow = jnp.sum(jnp.where(own_lane, h, 0.0), axis=0, keepdims=True)
        w = grow * _gelu(hrow)
        w_hi = w.astype(BF16).astype(F32)
        return jnp.where(keep, jnp.where(high_part, w_hi, w - w_hi), 0.0).astype(BF16)

    def mix_values(k, lhs):
        rhs = jnp.concatenate([pltpu.bitcast(words(k, half_chunks + c), BF16) for c in range(half_chunks)], axis=-1)
        o = _dot(lhs, rhs)
        return jnp.concatenate([o[0:1] + o[1:2], o[2:3] + o[3:4]], axis=-1)

    @pl.when(step == 0)
    def _():
        for k in range(MIX_AHEAD):
            issue(k, k)

    def group(g, carry):
        rows = pl.ds(pl.multiple_of(g * MIX_SLOTS, MIX_SLOTS), MIX_SLOTS)
        x_tile, g_tile = xf_ref[rows, :], gate_ref[rows, :]
        lhs_prev, yrows = None, []
        for k in range(MIX_SLOTS):
            wait(k)
            issue(g * MIX_SLOTS + k + MIX_AHEAD, (k + MIX_AHEAD) % MIX_SLOTS)
            lhs = gate_weights(k, x_tile[k:k + 1], g_tile[k:k + 1])
            if lhs_prev is not None:
                yrows.append(mix_values(k - 1, lhs_prev))
            lhs_prev = lhs
        yrows.append(mix_values(MIX_SLOTS - 1, lhs_prev))
        y_ref[rows, :] = _rms(x1_ref[rows, :] + jnp.concatenate(yrows, axis=0), gfin_ref[...])
        return carry

    lax.fori_loop(0, tt // MIX_SLOTS, group, 0)

    @pl.when(step == last)
    def _():
        for k in range(MIX_AHEAD):
            wait(k)


def _peer_mix(eid_t, gate_t, xf, x1, g_final, table, tt):
    n = xf.shape[0]
    steps = n // tt
    eid = eid_t.T
    look = jnp.concatenate([eid[tt:], eid[-tt:]], axis=0).reshape(steps, tt, PEER_PAIRS)[:, :MIX_SLOTS]
    ids = jnp.concatenate([eid.reshape(steps, tt, PEER_PAIRS), look], axis=1)
    gate = jnp.repeat(gate_t.T, 2, axis=1)
    row = pl.BlockSpec((tt, D_MODEL), lambda i: (i, 0))
    assert MIX_SLOTS == 8 and tt % MIX_SLOTS == 0
    return pl.pallas_call(
        functools.partial(_peer_mix_kernel, tt=tt),
        grid=(steps,),
        in_specs=[pl.BlockSpec((1, tt + MIX_SLOTS, PEER_PAIRS), lambda i: (i, 0, 0), memory_space=pltpu.SMEM),
                  pl.BlockSpec((tt, 2 * PEER_PAIRS), lambda i: (i, 0)),
                  row, row, _full((1, D_MODEL)),
                  pl.BlockSpec(memory_space=pl.ANY)],
        out_specs=row,
        out_shape=jax.ShapeDtypeStruct((n, D_MODEL), F32),
        scratch_shapes=[pltpu.VMEM((PEER_PAIRS * ROW_PITCH, LANES), WORD)] * MIX_SLOTS
        + [pltpu.SemaphoreType.DMA((MIX_SLOTS,))],
        compiler_params=_cparams(1), name="peer_mix",
    )(ids, gate, xf, x1, g_final, table)


def _rope_tables(pos):
    half = QK_ROPE // 2
    inv = ROPE_THETA ** (-jnp.arange(half, dtype=F32) / half)
    ang = pos.astype(F32)[:, None] * inv[None, :]
    cos, sin = jnp.cos(ang), jnp.sin(ang)
    return jnp.concatenate([cos, cos], -1), jnp.concatenate([sin, sin], -1)


def _rot_cols(w):
    half = w.shape[-1] // 2
    return jnp.concatenate([-w[..., half:], w[..., :half]], axis=-1)


def _group(x, tokens_per_seq, pos, cache, mem_kv, wts, g_final):
    n = x.shape[0]
    nseq = n // tokens_per_seq
    tm = min(256, n)
    cos2, sin2 = _rope_tables(pos)
    reps = max(tm // tokens_per_seq, 1)
    cos2, sin2 = jnp.tile(cos2, (reps, 1)), jnp.tile(sin2, (reps, 1))
    cos8, sin8 = jnp.tile(cos2, (1, MLA_HEADS)), jnp.tile(sin2, (1, MLA_HEADS))

    uv, c_q, ckv, krope, kcat, q_m = _proj(x, wts["g_attn"], wts["w_ext"], wts["g_kv"], cos2, sin2, tm)
    rows = min(GMLP_CHUNK, tokens_per_seq)
    ya, v = _gmlp(uv, wts["g_v"], wts["b_v"], wts["w_s"], wts["b_st"], wts["g_out_a"], rows)
    qcat = _mlaq(c_q, wts["g_q"], wts["w_uqp"], wts["w_ukt"], cos8, sin8, tm)
    if cache is None:
        yb = _mla_attn(qcat, kcat, jnp.swapaxes(wts["w_uvt"], 1, 2), wts["g_out_b"].reshape(MLA_DIM, 1), nseq,
                       tokens_per_seq, min(256, tokens_per_seq))
    else:
        yb = _mla_dec(qcat, cache[0], cache[1], kcat, wts["w_uvt"], wts["g_out_b"], nseq, tokens_per_seq)
    mk, mv = mem_kv
    tma = min(tm, tokens_per_seq)
    ym = _memattn(q_m, mk, mv, wts["g_out_m"], tma, tokens_per_seq // tma)
    x1, xf = _merge(x, ya, yb, ym, wts["w_out"], wts["g_ffn"], tm)
    eid_t, gate_t = _peer_topk(xf, wts["w_pq"], wts["sk1"], wts["sk2"], tm)
    y = _peer_mix(eid_t, gate_t, xf, x1, g_final, wts["table"], min(64, n))
    return y, ckv, krope, v


def kernel(x_prompt, x_sample, cache_mla_ckv, cache_mla_krope, cache_mem_k, cache_mem_v, mem_prompt, g_attn, w_in, g_v, b_v, w_s, b_s, g_q, w_uq, w_uk, w_uv, g_kv, g_mem, w_mk, w_mv, g_out_a, g_out_b, g_out_m, w_out, g_ffn, w_pq, sub_keys1, sub_keys2, peer_u, peer_v, g_final):
    assert w_in.shape[0] == 1, "the final norm is fused after the single layer"
    l = 0
    bp, sp, _ = x_prompt.shape
    bs, ts, _ = x_sample.shape
    past = cache_mla_ckv.shape[2]
    gfin = g_final.reshape(1, D_MODEL)
    wi = w_in[l]
    w_ext = jnp.concatenate([wi[:, :_C_KROT], _rot_cols(wi[:, _C_KR:_C_KROT]), wi[:, _C_KROT:]], axis=1)
    wq = w_uq[l].reshape(Q_LORA, MLA_HEADS, QK_NOPE + QK_ROPE)
    wq_rope = wq[:, :, QK_NOPE:]
    w_uqp = jnp.concatenate([wq[:, :, :QK_NOPE].reshape(Q_LORA, -1), wq_rope.reshape(Q_LORA, -1),
                             _rot_cols(wq_rope).reshape(Q_LORA, -1)], axis=1)
    wts = {
        "g_attn": g_attn[l].reshape(1, -1), "w_ext": w_ext.astype(BF16), "g_kv": g_kv[l].reshape(1, -1),
        "g_v": g_v[l].reshape(1, -1), "b_v": b_v[l].reshape(1, -1), "w_s": w_s[l], "b_st": b_s[l].T,
        "g_out_a": g_out_a[l].reshape(1, -1), "g_q": g_q[l].reshape(1, -1), "w_uqp": w_uqp.astype(BF16),
        "w_ukt": jnp.transpose(w_uk[l], (1, 2, 0)).astype(BF16),
        "w_uvt": jnp.transpose(w_uv[l], (1, 0, 2)).astype(BF16),
        "g_out_b": g_out_b[l].reshape(1, -1), "g_out_m": g_out_m[l].reshape(1, -1),
        "w_out": w_out[l].astype(BF16), "g_ffn": g_ffn[l].reshape(1, -1), "w_pq": w_pq[l].astype(BF16),
        "sk1": sub_keys1[l].astype(BF16), "sk2": sub_keys2[l].astype(BF16),
        "table": _peer_pack(peer_u[l], peer_v[l]),
    }
    mk, mv = _memkv(mem_prompt.reshape(bp * MEM_TOKENS, D_MODEL), g_mem[l].reshape(1, -1),
                    w_mk[l].astype(BF16), w_mv[l].astype(BF16))
    yp, ckv_p, kr_p, _ = _group(x_prompt.reshape(bp * sp, D_MODEL), sp, jnp.arange(sp), None, (mk, mv), wts, gfin)
    mem_s = (cache_mem_k[l].reshape(bs * MEM_TOKENS, MEM_DIM), cache_mem_v[l].reshape(bs * MEM_TOKENS, MEM_DIM))
    ys, ckv_s, kr_s, gv_s = _group(x_sample.reshape(bs * ts, D_MODEL), ts, past + jnp.arange(ts),
                                   (cache_mla_ckv[l], cache_mla_krope[l]), mem_s, wts, gfin)
    return (yp.reshape(bp, sp, D_MODEL), ys.reshape(bs, ts, D_MODEL),
            ckv_p.reshape(1, bp, sp, KV_LORA), kr_p.reshape(1, bp, sp, QK_ROPE),
            mk.reshape(1, bp, MEM_TOKENS, MEM_HEADS, MEM_HEAD_DIM),
            mv.reshape(1, bp, MEM_TOKENS, MEM_HEADS, MEM_HEAD_DIM),
            ckv_s.reshape(1, bs, ts, KV_LORA), kr_s.reshape(1, bs, ts, QK_ROPE),
            gv_s.reshape(1, bs, ts, GMLP_DIM))
```

```python
import functools
import math

import jax
import jax.numpy as jnp
from jax import lax
from jax.experimental import pallas as pl
from jax.experimental.pallas import tpu as pltpu

F32 = jnp.float32
BF16 = jnp.bfloat16
I32 = jnp.int32
WORD = jnp.uint32

D_MODEL = 2048
CHUNK = 64
CHUNK_SHIFT = 6
EPS = 1e-6
GMLP_CHUNK = 128
GMLP_DIM = 512
GMLP_GROUPS = 4
GMLP_GROUP_DIM = 128
V_HEAD = 128
QK_NOPE = 128
QK_ROPE = 64
MLA_HEADS = 8
MLA_DIM = 1024
Q_LORA = 512
KV_LORA = 256
ROPE_THETA = 10000.0
MLA_SCALE = (QK_NOPE + QK_ROPE) ** -0.5
QCAT = KV_LORA + QK_ROPE
MEM_TOKENS = 256
MEM_HEADS = 4
MEM_DIM = 512
MEM_HEAD_DIM = 128
MEM_SCALE = MEM_HEAD_DIM ** -0.5
PEER_HEADS = 8
N_KEYS = 128
PEER_QDIM = 256
PEER_HALF = 128
PEER_TOPK = 16
PEER_PAIRS = PEER_HEADS * PEER_TOPK
PAIR_ROWS_LISTED = 5
BIG_ORDER = 1e9
HALF_D = D_MODEL // 2
LANES = 128
ROW_CHUNKS = 2 * HALF_D // LANES
ROW_PITCH = 17
PACK_ROWS_MAX = 512
MIX_SLOTS = 8
MIX_AHEAD = 6

_C_UV, _C_Q, _C_KV, _C_KR, _C_KROT, _C_M, _C_END = 0, 1024, 1536, 1792, 1856, 1920, 2432

VMEM_LIMIT = 48 * 1024 * 1024


def _cparams(n_grid):
    return pltpu.CompilerParams(dimension_semantics=("arbitrary",) * n_grid,
                                vmem_limit_bytes=VMEM_LIMIT)


def _rms(x, g):
    return x * lax.rsqrt(jnp.mean(x * x, axis=-1, keepdims=True) + EPS) * g


def _gelu(x):
    return x * (0.5 * (1.0 + jnp.tanh(math.sqrt(2.0 / math.pi) * (x + 0.044715 * (x * x * x)))))


def _dot(a, b):
    return jnp.dot(a, b, preferred_element_type=F32)


def _dot_nt(a, b):
    return lax.dot_general(a, b, (((1,), (1,)), ((), ())), preferred_element_type=F32)


def _full(shape):
    n = len(shape)
    return pl.BlockSpec(shape, lambda *_: (0,) * n)


def _pack_rows(pu_ref, pv_ref, tab_ref, tr):
    for src, base in ((pu_ref, 0), (pv_ref, ROW_CHUNKS // 2)):
        x = src[...]
        words = pltpu.bitcast(pltpu.pack_elementwise([x[:, :HALF_D], x[:, HALF_D:]], packed_dtype=BF16), WORD)
        for c in range(ROW_CHUNKS // 2):
            tab_ref[pl.ds(base + c, tr, stride=ROW_CHUNKS), :] = words[:, c * LANES:(c + 1) * LANES]


def _gmlp_chunk(uv, gv_ref, bv_ref, ws_ref, bst_ref, goa_ref, rows):
    uv = _gelu(uv)
    u = uv[:, :GMLP_DIM]
    vr = uv[:, GMLP_DIM:]
    mu = jnp.mean(vr, axis=-1, keepdims=True)
    var = jnp.mean(jnp.square(vr - mu), axis=-1, keepdims=True)
    v = (vr - mu) * lax.rsqrt(var + EPS) * gv_ref[...] + bv_ref[...]
    r = lax.broadcasted_iota(I32, (rows, rows), 0)
    c = lax.broadcasted_iota(I32, (rows, rows), 1)
    zs = []
    for g in range(GMLP_GROUPS):
        w = jnp.where(r >= c, ws_ref[g, :rows, :rows], 0.0).astype(BF16)
        vg = v[:, g * GMLP_GROUP_DIM:(g + 1) * GMLP_GROUP_DIM].astype(BF16)
        zs.append(_dot(w, vg) + bst_ref[:rows, g:g + 1])
    y = u * jnp.concatenate(zs, axis=-1)
    return _rms(y, goa_ref[...]).astype(BF16), v


def _front_kernel(x_ref, g_ref, w_ref, gkv_ref, cos_ref, sin_ref, gv_ref, bv_ref, ws_ref, bst_ref, goa_ref,
                  gq_ref, wq_ref, wuk_ref, cos8_ref, sin8_ref,
                  ya_ref, v_ref, ckv_ref, kr_ref, kcat_ref, qm_ref, q_ref, *, tm, rows):
    xn = _rms(x_ref[...], g_ref[...])
    proj = _dot(xn.astype(BF16), w_ref[...])
    ckv = _rms(proj[:, _C_KV:_C_KR], gkv_ref[...])
    ckv_ref[...] = ckv
    krope = proj[:, _C_KR:_C_KROT] * cos_ref[...] + proj[:, _C_KROT:_C_M] * sin_ref[...]
    kr_ref[...] = krope
    kcat_ref[:, :KV_LORA] = ckv.astype(BF16)
    kcat_ref[:, KV_LORA:] = krope.astype(BF16)
    qm_ref[...] = proj[:, _C_M:_C_END]
    for r0 in range(0, tm, rows):
        ya, v = _gmlp_chunk(proj[r0:r0 + rows, _C_UV:_C_Q], gv_ref, bv_ref, ws_ref, bst_ref, goa_ref, rows)
        ya_ref[r0:r0 + rows, :] = ya
        v_ref[r0:r0 + rows, :] = v
    cq = _rms(proj[:, _C_Q:_C_KV], gq_ref[...])
    q = _dot(cq.astype(BF16), wq_ref[...])
    nr = MLA_HEADS * QK_NOPE
    rw = MLA_HEADS * QK_ROPE
    qrope = q[:, nr:nr + rw] * cos8_ref[...] + q[:, nr + rw:] * sin8_ref[...]
    for h in range(MLA_HEADS):
        qn = q[:, h * QK_NOPE:(h + 1) * QK_NOPE].astype(BF16)
        q_ref[h, :, :KV_LORA] = _dot(qn, wuk_ref[h]).astype(BF16)
        q_ref[h, :, KV_LORA:] = qrope[:, h * QK_ROPE:(h + 1) * QK_ROPE].astype(BF16)


def _front(x, wts, cos2, sin2, cos8, sin8, tm, rows):
    n = x.shape[0]
    nper = cos2.shape[0] // tm
    row = lambda w: pl.BlockSpec((tm, w), lambda i: (i, 0))
    tab = lambda w: pl.BlockSpec((tm, w), lambda i: (i % nper, 0))
    consts = [wts[k] for k in ("g_attn", "w_ext", "g_kv")]
    consts_a = [wts[k] for k in ("g_v", "b_v", "w_s", "b_st", "g_out_a")]
    consts_q = [wts[k] for k in ("g_q", "w_uqp", "w_ukt")]
    full = lambda arrs: [_full(a.shape) for a in arrs]
    sds = jax.ShapeDtypeStruct
    return pl.pallas_call(
        functools.partial(_front_kernel, tm=tm, rows=rows),
        grid=(n // tm,),
        in_specs=[row(D_MODEL)] + full(consts) + [tab(QK_ROPE)] * 2 + full(consts_a) + full(consts_q)
        + [tab(MLA_HEADS * QK_ROPE)] * 2,
        out_specs=[row(GMLP_DIM), row(GMLP_DIM), row(KV_LORA), row(QK_ROPE), row(QCAT), row(MEM_DIM),
                   pl.BlockSpec((MLA_HEADS, tm, QCAT), lambda i: (0, i, 0))],
        out_shape=[sds((n, GMLP_DIM), BF16), sds((n, GMLP_DIM), F32), sds((n, KV_LORA), F32), sds((n, QK_ROPE), F32),
                   sds((n, QCAT), BF16), sds((n, MEM_DIM), F32), sds((MLA_HEADS, n, QCAT), BF16)],
        compiler_params=_cparams(1), name="front",
    )(x, *consts, cos2, sin2, *consts_a, *consts_q, cos8, sin8)


def _mla_finish(o, wuv_ref, gob_ref, tq):
    ys = [_dot(o[h * tq:(h + 1) * tq].astype(BF16), wuv_ref[h]) for h in range(MLA_HEADS)]
    return _rms(jnp.concatenate(ys, axis=-1), gob_ref[...]).astype(BF16)


def _mla_attn_kernel(q_ref, k_ref, wuvt_ref, gob_ref, *refs, tq, pack_rows):
    if pack_rows:
        pu_ref, pv_ref, yb_ref, tab_ref, m_ref, l_ref, acc_ref = refs
    else:
        yb_ref, m_ref, l_ref, acc_ref = refs
    i = pl.program_id(1)
    cols = MLA_HEADS * tq
    q = q_ref[...].reshape(cols, QCAT)
    m_ref[...] = jnp.full((1, cols), -jnp.inf, F32)
    l_ref[...] = jnp.zeros((1, cols), F32)
    acc_ref[...] = jnp.zeros((KV_LORA, cols), F32)
    key_chunk = lax.broadcasted_iota(I32, (tq, tq), 0) >> CHUNK_SHIFT
    qry_chunk = lax.broadcasted_iota(I32, (tq, tq), 1) >> CHUNK_SHIFT
    allowed = jnp.concatenate([key_chunk <= qry_chunk] * MLA_HEADS, axis=1)

    def block(j, diagonal):
        k = k_ref[pl.ds(pl.multiple_of(j * tq, tq), tq), :]
        v_t = k[:, :KV_LORA].astype(F32).T.astype(BF16)
        s = _dot_nt(k, q) * MLA_SCALE
        if diagonal:
            s = jnp.where(allowed, s, -jnp.inf)
        m_old = m_ref[...]
        m_new = jnp.maximum(m_old, jnp.max(s, axis=0, keepdims=True))
        alpha = jnp.exp(m_old - m_new)
        p = jnp.exp(s - m_new)
        l_ref[...] = alpha * l_ref[...] + jnp.sum(p, axis=0, keepdims=True)
        acc_ref[...] = alpha * acc_ref[...] + _dot(v_t, p.astype(BF16))
        m_ref[...] = m_new

    def body(j, carry):
        block(j, False)
        return carry

    lax.fori_loop(0, i, body, 0)
    if pack_rows:
        _pack_rows(pu_ref, pv_ref, tab_ref, pack_rows)
    block(i, True)
    o = (acc_ref[...] / l_ref[...]).astype(BF16)
    ys = [_dot(wuvt_ref[h], o[:, h * tq:(h + 1) * tq]) for h in range(MLA_HEADS)]
    y = jnp.concatenate(ys, axis=0)
    y = y * lax.rsqrt(jnp.mean(y * y, axis=0, keepdims=True) + EPS) * gob_ref[...]
    yb_ref[...] = y.T.astype(BF16)


def _mla_attn(qcat, kcat, w_uvtt, g_out_b_col, peer_u, peer_v, batch, seq, tq):
    n = batch * seq
    nq = seq // tq
    ne = peer_u.shape[0]
    pack_rows = ne // (batch * nq) if ne % (batch * nq) == 0 and ne // (batch * nq) <= PACK_ROWS_MAX else 0
    in_specs = [pl.BlockSpec((MLA_HEADS, tq, QCAT), lambda b, i: (0, b * nq + i, 0)),
                pl.BlockSpec((seq, QCAT), lambda b, i: (b, 0)),
                _full(w_uvtt.shape), _full((MLA_DIM, 1))]
    out_specs = [pl.BlockSpec((tq, MLA_DIM), lambda b, i: (b * nq + i, 0))]
    out_shape = [jax.ShapeDtypeStruct((n, MLA_DIM), BF16)]
    args = [qcat, kcat, w_uvtt, g_out_b_col]
    if pack_rows:
        in_specs += [pl.BlockSpec((pack_rows, D_MODEL), lambda b, i: (b * nq + i, 0))] * 2
        out_specs.append(pl.BlockSpec((pack_rows * ROW_CHUNKS, LANES), lambda b, i: (b * nq + i, 0)))
        out_shape.append(jax.ShapeDtypeStruct((ne * ROW_CHUNKS, LANES), WORD))
        args += [peer_u, peer_v]
    outs = pl.pallas_call(
        functools.partial(_mla_attn_kernel, tq=tq, pack_rows=pack_rows),
        grid=(batch, nq),
        in_specs=in_specs, out_specs=out_specs, out_shape=out_shape,
        scratch_shapes=[pltpu.VMEM((1, MLA_HEADS * tq), F32), pltpu.VMEM((1, MLA_HEADS * tq), F32),
                        pltpu.VMEM((KV_LORA, MLA_HEADS * tq), F32)],
        compiler_params=_cparams(2), name="mla_attn",
    )(*args)
    return outs[0], (outs[1] if pack_rows else _peer_pack(peer_u, peer_v))


def _mla_dec_kernel(q_ref, cc_ref, ck_ref, kn_ref, wuv_ref, gob_ref, yb_ref, *, t):
    rows = MLA_HEADS * t
    q = q_ref[...].reshape(rows, QCAT)
    cc = cc_ref[0].astype(BF16)
    ck = ck_ref[0].astype(BF16)
    kn = kn_ref[...]
    s_c = (_dot_nt(q[:, :KV_LORA], cc) + _dot_nt(q[:, KV_LORA:], ck)) * MLA_SCALE
    s_n = _dot_nt(q, kn) * MLA_SCALE
    m = jnp.maximum(jnp.max(s_c, axis=-1, keepdims=True), jnp.max(s_n, axis=-1, keepdims=True))
    p_c = jnp.exp(s_c - m)
    p_n = jnp.exp(s_n - m)
    l = jnp.sum(p_c, axis=-1, keepdims=True) + jnp.sum(p_n, axis=-1, keepdims=True)
    o = (_dot(p_c.astype(BF16), cc) + _dot(p_n.astype(BF16), kn[:, :KV_LORA])) / l
    yb_ref[...] = _mla_finish(o, wuv_ref, gob_ref, t)


def _mla_dec(qcat, cache_ckv, cache_krope, kcat, w_uvt, g_out_b, batch, t):
    past = cache_ckv.shape[1]
    return pl.pallas_call(
        functools.partial(_mla_dec_kernel, t=t),
        grid=(batch,),
        in_specs=[pl.BlockSpec((MLA_HEADS, t, QCAT), lambda b: (0, b, 0)),
                  pl.BlockSpec((1, past, KV_LORA), lambda b: (b, 0, 0)),
                  pl.BlockSpec((1, past, QK_ROPE), lambda b: (b, 0, 0)),
                  pl.BlockSpec((t, QCAT), lambda b: (b, 0)),
                  _full(w_uvt.shape), _full((1, MLA_DIM))],
        out_specs=pl.BlockSpec((t, MLA_DIM), lambda b: (b, 0)),
        out_shape=jax.ShapeDtypeStruct((batch * t, MLA_DIM), BF16),
        compiler_params=_cparams(1), name="mla_dec",
    )(qcat, cache_ckv, cache_krope, kcat, w_uvt, g_out_b)


def _memkv_kernel(mem_ref, g_ref, wk_ref, wv_ref, mk_ref, mv_ref):
    mn = _rms(mem_ref[...], g_ref[...]).astype(BF16)
    mk_ref[...] = _dot(mn, wk_ref[...])
    mv_ref[...] = _dot(mn, wv_ref[...])


def _memkv(mem, g_mem, w_mk, w_mv):
    n = mem.shape[0]
    tm = MEM_TOKENS
    return pl.pallas_call(
        _memkv_kernel,
        grid=(n // tm,),
        in_specs=[pl.BlockSpec((tm, D_MODEL), lambda i: (i, 0)), _full((1, D_MODEL)),
                  _full(w_mk.shape), _full(w_mv.shape)],
        out_specs=[pl.BlockSpec((tm, MEM_DIM), lambda i: (i, 0))] * 2,
        out_shape=[jax.ShapeDtypeStruct((n, MEM_DIM), F32)] * 2,
        compiler_params=_cparams(1), name="mem_kv",
    )(mem, g_mem, w_mk, w_mv)


def _memattn_kernel(qm_ref, mk_ref, mv_ref, gom_ref, ym_ref):
    qm = qm_ref[...]
    outs = []
    for h in range(MEM_HEADS):
        sl = slice(h * MEM_HEAD_DIM, (h + 1) * MEM_HEAD_DIM)
        s = _dot_nt(qm[:, sl].astype(BF16), mk_ref[:, sl].astype(BF16)) * MEM_SCALE
        e = jnp.exp(s - jnp.max(s, axis=-1, keepdims=True))
        p = e / jnp.sum(e, axis=-1, keepdims=True)
        outs.append(_dot(p.astype(BF16), mv_ref[:, sl].astype(BF16)))
    ym_ref[...] = _rms(jnp.concatenate(outs, axis=-1), gom_ref[...]).astype(BF16)


def _memattn(q_m, mk, mv, g_out_m, tm, tiles_per_batch):
    n = q_m.shape[0]
    kv = pl.BlockSpec((MEM_TOKENS, MEM_DIM), lambda i: (i // tiles_per_batch, 0))
    return pl.pallas_call(
        _memattn_kernel,
        grid=(n // tm,),
        in_specs=[pl.BlockSpec((tm, MEM_DIM), lambda i: (i, 0)), kv, kv, _full((1, MEM_DIM))],
        out_specs=pl.BlockSpec((tm, MEM_DIM), lambda i: (i, 0)),
        out_shape=jax.ShapeDtypeStruct((n, MEM_DIM), BF16),
        compiler_params=_cparams(1), name="mem_attn",
    )(q_m, mk, mv, g_out_m)


def _merge_kernel(x_ref, ya_ref, yb_ref, ym_ref, w_ref, gf_ref, x1_ref, xf_ref):
    a0, a1 = GMLP_DIM, GMLP_DIM + MLA_DIM
    y = (_dot(ya_ref[...], w_ref[:a0, :]) + _dot(yb_ref[...], w_ref[a0:a1, :])
         + _dot(ym_ref[...], w_ref[a1:, :]))
    x1 = x_ref[...] + y
    x1_ref[...] = x1
    xf_ref[...] = _rms(x1, gf_ref[...])


def _merge(x, ya, yb, ym, w_out, g_ffn, tm):
    n = x.shape[0]
    row = lambda w: pl.BlockSpec((tm, w), lambda i: (i, 0))
    return pl.pallas_call(
        _merge_kernel,
        grid=(n // tm,),
        in_specs=[row(D_MODEL), row(GMLP_DIM), row(MLA_DIM), row(MEM_DIM), _full(w_out.shape),
                  _full((1, D_MODEL))],
        out_specs=[row(D_MODEL), row(D_MODEL)],
        out_shape=[jax.ShapeDtypeStruct((n, D_MODEL), F32)] * 2,
        compiler_params=_cparams(1), name="merge",
    )(x, ya, yb, ym, w_out, g_ffn)


def _top16(s, order, payload):
    vals, picks = [], []
    for _ in range(PEER_TOPK):
        m = jnp.max(s, axis=0, keepdims=True)
        first = jnp.min(jnp.where(s == m, order, BIG_ORDER), axis=0, keepdims=True)
        sel = order == first
        vals.append(m)
        picks.append(first if payload is None else jnp.max(jnp.where(sel, payload, -1.0), axis=0, keepdims=True))
        s = jnp.where(sel, -jnp.inf, s)
    return jnp.concatenate(vals, axis=0), jnp.concatenate(picks, axis=0)


def _pair_candidates(v1, i1, v2, i2):
    tm = v1.shape[1]
    row = lax.broadcasted_iota(I32, (PEER_TOPK, tm), 0)
    top = row < 8
    low = (row & 7).astype(F32)
    rowf = row.astype(F32)
    halves = lambda x: jnp.where(top, x, pltpu.roll(x, 8, 0))
    v1h, i1h, v2h, i2h = halves(v1), halves(i1), halves(v2), halves(i2)
    cand = [v1[0:1] + v2]
    eid = [i1[0:1] * N_KEYS + i2]
    order = [rowf]
    for a0, a1 in ((1, 2), (3, 4)):
        cand.append(jnp.where(top, v1[a0:a0 + 1], v1[a1:a1 + 1]) + v2h)
        eid.append(jnp.where(top, i1[a0:a0 + 1], i1[a1:a1 + 1]) * N_KEYS + i2h)
        order.append(jnp.where(top, a0 * PEER_TOPK, a1 * PEER_TOPK) + low)
    listed = rowf < PAIR_ROWS_LISTED
    cand.append(jnp.where(listed, -jnp.inf, v1 + v2[0:1]))
    eid.append(i1 * N_KEYS + i2[0:1])
    order.append(jnp.where(listed, BIG_ORDER, rowf * PEER_TOPK))
    listed = low < PAIR_ROWS_LISTED
    cand.append(jnp.where(listed, -jnp.inf, v1h + jnp.where(top, v2[1:2], v2[2:3])))
    eid.append(i1h * N_KEYS + jnp.where(top, i2[1:2], i2[2:3]))
    order.append(jnp.where(listed, BIG_ORDER, low * PEER_TOPK + jnp.where(top, 1.0, 2.0)))
    return jnp.concatenate(cand, axis=0), jnp.concatenate(order, axis=0), jnp.concatenate(eid, axis=0)


def _peer_topk_kernel(xf_ref, wpq_ref, sk1_ref, sk2_ref, eid_ref, gate_ref, *, tm):
    q = _dot(xf_ref[...].astype(BF16), wpq_ref[...])
    key_order = lax.broadcasted_iota(I32, (N_KEYS, tm), 0).astype(F32)
    for h in range(PEER_HEADS):
        qa = q[:, h * PEER_QDIM:h * PEER_QDIM + PEER_HALF].astype(BF16)
        qb = q[:, h * PEER_QDIM + PEER_HALF:(h + 1) * PEER_QDIM].astype(BF16)
        v1, i1 = _top16(_dot_nt(sk1_ref[...], qa), key_order, None)
        v2, i2 = _top16(_dot_nt(sk2_ref[...], qb), key_order, None)
        vals, eid = _top16(*_pair_candidates(v1, i1, v2, i2))
        e = jnp.exp(vals - vals[0:1])
        rows = slice(h * PEER_TOPK, (h + 1) * PEER_TOPK)
        gate_ref[rows, :] = e / jnp.sum(e, axis=0, keepdims=True)
        eid_ref[rows, :] = eid.astype(I32)


def _peer_topk(xf, w_pq, sk1, sk2, tm):
    n = xf.shape[0]
    out = pl.BlockSpec((PEER_PAIRS, tm), lambda i: (0, i))
    return pl.pallas_call(
        functools.partial(_peer_topk_kernel, tm=tm),
        grid=(n // tm,),
        in_specs=[pl.BlockSpec((tm, D_MODEL), lambda i: (i, 0)), _full(w_pq.shape),
                  _full(sk1.shape), _full(sk2.shape)],
        out_specs=[out, out],
        out_shape=[jax.ShapeDtypeStruct((PEER_PAIRS, n), I32), jax.ShapeDtypeStruct((PEER_PAIRS, n), F32)],
        compiler_params=_cparams(1), name="peer_topk",
    )(xf, w_pq, sk1, sk2)


def _peer_pack_kernel(pu_ref, pv_ref, tab_ref, *, tr):
    _pack_rows(pu_ref, pv_ref, tab_ref, tr)


def _peer_pack(peer_u, peer_v):
    ne = peer_u.shape[0]
    tr = 256
    return pl.pallas_call(
        functools.partial(_peer_pack_kernel, tr=tr),
        grid=(ne // tr,),
        in_specs=[pl.BlockSpec((tr, D_MODEL), lambda i: (i, 0))] * 2,
        out_specs=pl.BlockSpec((tr * ROW_CHUNKS, LANES), lambda i: (i, 0)),
        out_shape=jax.ShapeDtypeStruct((ne * ROW_CHUNKS, LANES), WORD),
        compiler_params=_cparams(1), name="peer_pack",
    )(peer_u, peer_v)


def _unpack(words):
    return tuple(pltpu.unpack_elementwise(words, index=i, packed_dtype=BF16, unpacked_dtype=F32) for i in (0, 1))


def _peer_mix_kernel(ids_ref, gate_ref, xf_ref, x1_ref, gfin_ref, tab_ref, y_ref, *scratch, tt):
    bufs, sems = scratch[:MIX_SLOTS], scratch[MIX_SLOTS]
    half_chunks = ROW_CHUNKS // 2
    step, last = pl.program_id(0), pl.num_programs(0) - 1

    def issue(t, k):
        for p in range(PEER_PAIRS):
            row0 = pl.multiple_of(ids_ref[0, t, p] * ROW_CHUNKS, ROW_CHUNKS)
            pltpu.make_async_copy(tab_ref.at[pl.ds(row0, ROW_CHUNKS)],
                                  bufs[k].at[pl.ds(p * ROW_PITCH, ROW_CHUNKS)],
                                  sems.at[k]).start(priority=p % 2)

    def wait(k):
        n = PEER_PAIRS * ROW_CHUNKS
        pltpu.make_async_copy(tab_ref.at[pl.ds(0, n)], bufs[k].at[pl.ds(0, n)], sems.at[k]).wait()

    def words(k, c):
        return bufs[k][pl.ds(c, PEER_PAIRS, stride=ROW_PITCH), :]

    pair_of_lane = lax.broadcasted_iota(I32, (PEER_PAIRS, 2 * PEER_PAIRS), 1) >> 1
    own_lane = pair_of_lane == lax.broadcasted_iota(I32, (PEER_PAIRS, 2 * PEER_PAIRS), 0)
    sub = lax.broadcasted_iota(I32, (8, 2 * PEER_PAIRS), 0)
    lane = lax.broadcasted_iota(I32, (8, 2 * PEER_PAIRS), 1)
    keep = (sub < 4) & ((lane & 1) == (sub >> 1))
    high_part = (sub & 1) == 0

    def gate_weights(k, xrow, grow):
        acc = jnp.zeros((PEER_PAIRS, LANES), F32)
        for c in range(half_chunks):
            lo, hi = _unpack(words(k, c))
            acc = acc + lo * xrow[:, c * LANES:(c + 1) * LANES]
            acc = acc + hi * xrow[:, HALF_D + c * LANES:HALF_D + (c + 1) * LANES]
        h = jnp.sum(acc, axis=-1, keepdims=True)
        hrow = jnp.sum(jnp.where(own_lane, h, 0.0), axis=0, keepdims=True)
        w = grow * _gelu(hrow)
        w_hi = w.astype(BF16).astype(F32)
        return jnp.where(keep, jnp.where(high_part, w_hi, w - w_hi), 0.0).astype(BF16)

    def mix_values(k, lhs):
        rhs = jnp.concatenate([pltpu.bitcast(words(k, half_chunks + c), BF16) for c in range(half_chunks)], axis=-1)
        o = _dot(lhs, rhs)
        return jnp.concatenate([o[0:1] + o[1:2], o[2:3] + o[3:4]], axis=-1)

    @pl.when(step == 0)
    def _():
        for k in range(MIX_AHEAD):
            issue(k, k)

    def group(g, carry):
        rows = pl.ds(pl.multiple_of(g * MIX_SLOTS, MIX_SLOTS), MIX_SLOTS)
        x_tile, g_tile = xf_ref[rows, :], gate_ref[rows, :]
        lhs_prev, yrows = None, []
        for k in range(MIX_SLOTS):
            wait(k)
            issue(g * MIX_SLOTS + k + MIX_AHEAD, (k + MIX_AHEAD) % MIX_SLOTS)
            lhs = gate_weights(k, x_tile[k:k + 1], g_tile[k:k + 1])
            if lhs_prev is not None:
                yrows.append(mix_values(k - 1, lhs_prev))
            lhs_prev = lhs
        yrows.append(mix_values(MIX_SLOTS - 1, lhs_prev))
        y_ref[rows, :] = _rms(x1_ref[rows, :] + jnp.concatenate(yrows, axis=0), gfin_ref[...])
        return carry

    lax.fori_loop(0, tt // MIX_SLOTS, group, 0)

    @pl.when(step == last)
    def _():
        for k in range(MIX_AHEAD):
            wait(k)


def _peer_mix(eid_t, gate_t, xf, x1, g_final, table, tt):
    n = xf.shape[0]
    steps = n // tt
    eid = eid_t.T
    look = jnp.concatenate([eid[tt:], eid[-tt:]], axis=0).reshape(steps, tt, PEER_PAIRS)[:, :MIX_SLOTS]
    ids = jnp.concatenate([eid.reshape(steps, tt, PEER_PAIRS), look], axis=1)
    gate = jnp.repeat(gate_t.T, 2, axis=1)
    row = pl.BlockSpec((tt, D_MODEL), lambda i: (i, 0))
    assert MIX_SLOTS == 8 and tt % MIX_SLOTS == 0
    return pl.pallas_call(
        functools.partial(_peer_mix_kernel, tt=tt),
        grid=(steps,),
        in_specs=[pl.BlockSpec((1, tt + MIX_SLOTS, PEER_PAIRS), lambda i: (i, 0, 0), memory_space=pltpu.SMEM),
                  pl.BlockSpec((tt, 2 * PEER_PAIRS), lambda i: (i, 0)),
                  row, row, _full((1, D_MODEL)),
                  pl.BlockSpec(memory_space=pl.ANY)],
        out_specs=row,
        out_shape=jax.ShapeDtypeStruct((n, D_MODEL), F32),
        scratch_shapes=[pltpu.VMEM((PEER_PAIRS * ROW_PITCH, LANES), WORD)] * MIX_SLOTS
        + [pltpu.SemaphoreType.DMA((MIX_SLOTS,))],
        compiler_params=_cparams(1), name="peer_mix",
    )(ids, gate, xf, x1, g_final, table)


def _rope_tables(pos):
    half = QK_ROPE // 2
    inv = ROPE_THETA ** (-jnp.arange(half, dtype=F32) / half)
    ang = pos.astype(F32)[:, None] * inv[None, :]
    cos, sin = jnp.cos(ang), jnp.sin(ang)
    return jnp.concatenate([cos, cos], -1), jnp.concatenate([sin, sin], -1)


def _rot_cols(w):
    half = w.shape[-1] // 2
    return jnp.concatenate([-w[..., half:], w[..., :half]], axis=-1)


def _group(x, tokens_per_seq, pos, cache, mem_kv, wts, g_final):
    n = x.shape[0]
    nseq = n // tokens_per_seq
    tm = min(256, n)
    cos2, sin2 = _rope_tables(pos)
    reps = max(tm // tokens_per_seq, 1)
    cos2, sin2 = jnp.tile(cos2, (reps, 1)), jnp.tile(sin2, (reps, 1))
    cos8, sin8 = jnp.tile(cos2, (1, MLA_HEADS)), jnp.tile(sin2, (1, MLA_HEADS))

    ya, v, ckv, krope, kcat, q_m, qcat = _front(x, wts, cos2, sin2, cos8, sin8, tm, min(GMLP_CHUNK, tokens_per_seq))
    if cache is None:
        yb, table = _mla_attn(qcat, kcat, jnp.swapaxes(wts["w_uvt"], 1, 2), wts["g_out_b"].reshape(MLA_DIM, 1),
                              wts["peer_u"], wts["peer_v"], nseq, tokens_per_seq, min(256, tokens_per_seq))
    else:
        yb = _mla_dec(qcat, cache[0], cache[1], kcat, wts["w_uvt"], wts["g_out_b"], nseq, tokens_per_seq)
        table = wts["table"]
    mk, mv = mem_kv
    tma = min(tm, tokens_per_seq)
    ym = _memattn(q_m, mk, mv, wts["g_out_m"], tma, tokens_per_seq // tma)
    x1, xf = _merge(x, ya, yb, ym, wts["w_out"], wts["g_ffn"], tm)
    eid_t, gate_t = _peer_topk(xf, wts["w_pq"], wts["sk1"], wts["sk2"], tm)
    y = _peer_mix(eid_t, gate_t, xf, x1, g_final, table, min(64, n))
    return y, ckv, krope, v, table


def kernel(x_prompt, x_sample, cache_mla_ckv, cache_mla_krope, cache_mem_k, cache_mem_v, mem_prompt, g_attn, w_in, g_v, b_v, w_s, b_s, g_q, w_uq, w_uk, w_uv, g_kv, g_mem, w_mk, w_mv, g_out_a, g_out_b, g_out_m, w_out, g_ffn, w_pq, sub_keys1, sub_keys2, peer_u, peer_v, g_final):
    assert w_in.shape[0] == 1, "the final norm is fused after the single layer"
    l = 0
    bp, sp, _ = x_prompt.shape
    bs, ts, _ = x_sample.shape
    past = cache_mla_ckv.shape[2]
    gfin = g_final.reshape(1, D_MODEL)
    wi = w_in[l]
    w_ext = jnp.concatenate([wi[:, :_C_KROT], _rot_cols(wi[:, _C_KR:_C_KROT]), wi[:, _C_KROT:]], axis=1)
    wq = w_uq[l].reshape(Q_LORA, MLA_HEADS, QK_NOPE + QK_ROPE)
    wq_rope = wq[:, :, QK_NOPE:]
    w_uqp = jnp.concatenate([wq[:, :, :QK_NOPE].reshape(Q_LORA, -1), wq_rope.reshape(Q_LORA, -1),
                             _rot_cols(wq_rope).reshape(Q_LORA, -1)], axis=1)
    wts = {
        "g_attn": g_attn[l].reshape(1, -1), "w_ext": w_ext.astype(BF16), "g_kv": g_kv[l].reshape(1, -1),
        "g_v": g_v[l].reshape(1, -1), "b_v": b_v[l].reshape(1, -1), "w_s": w_s[l], "b_st": b_s[l].T,
        "g_out_a": g_out_a[l].reshape(1, -1), "g_q": g_q[l].reshape(1, -1), "w_uqp": w_uqp.astype(BF16),
        "w_ukt": jnp.transpose(w_uk[l], (1, 2, 0)).astype(BF16),
        "w_uvt": jnp.transpose(w_uv[l], (1, 0, 2)).astype(BF16),
        "g_out_b": g_out_b[l].reshape(1, -1), "g_out_m": g_out_m[l].reshape(1, -1),
        "w_out": w_out[l].astype(BF16), "g_ffn": g_ffn[l].reshape(1, -1), "w_pq": w_pq[l].astype(BF16),
        "sk1": sub_keys1[l].astype(BF16), "sk2": sub_keys2[l].astype(BF16),
        "peer_u": peer_u[l], "peer_v": peer_v[l],
    }
    mk, mv = _memkv(mem_prompt.reshape(bp * MEM_TOKENS, D_MODEL), g_mem[l].reshape(1, -1),
                    w_mk[l].astype(BF16), w_mv[l].astype(BF16))
    yp, ckv_p, kr_p, _, wts["table"] = _group(x_prompt.reshape(bp * sp, D_MODEL), sp, jnp.arange(sp), None, (mk, mv),
                                              wts, gfin)
    mem_s = (cache_mem_k[l].reshape(bs * MEM_TOKENS, MEM_DIM), cache_mem_v[l].reshape(bs * MEM_TOKENS, MEM_DIM))
    ys, ckv_s, kr_s, gv_s, _ = _group(x_sample.reshape(bs * ts, D_MODEL), ts, past + jnp.arange(ts),
                                      (cache_mla_ckv[l], cache_mla_krope[l]), mem_s, wts, gfin)
    return (yp.reshape(bp, sp, D_MODEL), ys.reshape(bs, ts, D_MODEL),
            ckv_p.reshape(1, bp, sp, KV_LORA), kr_p.reshape(1, bp, sp, QK_ROPE),
            mk.reshape(1, bp, MEM_TOKENS, MEM_HEADS, MEM_HEAD_DIM),
            mv.reshape(1, bp, MEM_TOKENS, MEM_HEADS, MEM_HEAD_DIM),
            ckv_s.reshape(1, bs, ts, KV_LORA), kr_s.reshape(1, bs, ts, QK_ROPE),
            gv_s.reshape(1, bs, ts, GMLP_DIM))
```

```python
import functools
import math

import jax
import jax.numpy as jnp
from jax import lax
from jax.experimental import pallas as pl
from jax.experimental.pallas import tpu as pltpu

F32 = jnp.float32
BF16 = jnp.bfloat16
I32 = jnp.int32
WORD = jnp.uint32

D_MODEL = 2048
CHUNK = 64
CHUNK_SHIFT = 6
EPS = 1e-6
GMLP_CHUNK = 128
GMLP_DIM = 512
GMLP_GROUPS = 4
GMLP_GROUP_DIM = 128
V_HEAD = 128
QK_NOPE = 128
QK_ROPE = 64
MLA_HEADS = 8
MLA_DIM = 1024
Q_LORA = 512
KV_LORA = 256
ROPE_THETA = 10000.0
MLA_SCALE = (QK_NOPE + QK_ROPE) ** -0.5
QCAT = KV_LORA + QK_ROPE
MEM_TOKENS = 256
MEM_HEADS = 4
MEM_DIM = 512
MEM_HEAD_DIM = 128
MEM_SCALE = MEM_HEAD_DIM ** -0.5
PEER_HEADS = 8
N_KEYS = 128
PEER_QDIM = 256
PEER_HALF = 128
PEER_TOPK = 16
PEER_PAIRS = PEER_HEADS * PEER_TOPK
PAIR_ROWS_LISTED = 5
BIG_ORDER = 1e9
HALF_D = D_MODEL // 2
LANES = 128
ROW_CHUNKS = 2 * HALF_D // LANES
ROW_PITCH = 17
PACK_ROWS_MAX = 512
MIX_SLOTS = 8
MIX_AHEAD = 6

_C_UV, _C_Q, _C_KV, _C_KR, _C_KROT, _C_M, _C_END = 0, 1024, 1536, 1792, 1856, 1920, 2432

VMEM_LIMIT = 48 * 1024 * 1024


def _cparams(n_grid):
    return pltpu.CompilerParams(dimension_semantics=("arbitrary",) * n_grid,
                                vmem_limit_bytes=VMEM_LIMIT)


def _rms(x, g):
    return x * lax.rsqrt(jnp.mean(x * x, axis=-1, keepdims=True) + EPS) * g


def _gelu(x):
    return x * (0.5 * (1.0 + jnp.tanh(math.sqrt(2.0 / math.pi) * (x + 0.044715 * (x * x * x)))))


def _dot(a, b):
    return jnp.dot(a, b, preferred_element_type=F32)


def _dot_nt(a, b):
    return lax.dot_general(a, b, (((1,), (1,)), ((), ())), preferred_element_type=F32)


def _full(shape):
    n = len(shape)
    return pl.BlockSpec(shape, lambda *_: (0,) * n)


def _pack_rows(pu_ref, pv_ref, tab_ref, tr):
    for src, base in ((pu_ref, 0), (pv_ref, ROW_CHUNKS // 2)):
        x = src[...]
        words = pltpu.bitcast(pltpu.pack_elementwise([x[:, :HALF_D], x[:, HALF_D:]], packed_dtype=BF16), WORD)
        for c in range(ROW_CHUNKS // 2):
            tab_ref[pl.ds(base + c, tr, stride=ROW_CHUNKS), :] = words[:, c * LANES:(c + 1) * LANES]


def _gmlp_chunk(uv, gv_ref, bv_ref, ws_ref, bst_ref, goa_ref, rows):
    uv = _gelu(uv)
    u = uv[:, :GMLP_DIM]
    vr = uv[:, GMLP_DIM:]
    mu = jnp.mean(vr, axis=-1, keepdims=True)
    var = jnp.mean(jnp.square(vr - mu), axis=-1, keepdims=True)
    v = (vr - mu) * lax.rsqrt(var + EPS) * gv_ref[...] + bv_ref[...]
    r = lax.broadcasted_iota(I32, (rows, rows), 0)
    c = lax.broadcasted_iota(I32, (rows, rows), 1)
    zs = []
    for g in range(GMLP_GROUPS):
        w = jnp.where(r >= c, ws_ref[g, :rows, :rows], 0.0).astype(BF16)
        vg = v[:, g * GMLP_GROUP_DIM:(g + 1) * GMLP_GROUP_DIM].astype(BF16)
        zs.append(_dot(w, vg) + bst_ref[:rows, g:g + 1])
    y = u * jnp.concatenate(zs, axis=-1)
    return _rms(y, goa_ref[...]).astype(BF16), v


def _front_kernel(x_ref, g_ref, w_ref, gkv_ref, cos_ref, sin_ref, gv_ref, bv_ref, ws_ref, bst_ref, goa_ref,
                  gq_ref, wq_ref, wuk_ref, cos8_ref, sin8_ref,
                  ya_ref, v_ref, ckv_ref, kr_ref, kcat_ref, qm_ref, q_ref, *, tm, rows):
    xn = _rms(x_ref[...], g_ref[...])
    proj = _dot(xn.astype(BF16), w_ref[...])
    ckv = _rms(proj[:, _C_KV:_C_KR], gkv_ref[...])
    ckv_ref[...] = ckv
    krope = proj[:, _C_KR:_C_KROT] * cos_ref[...] + proj[:, _C_KROT:_C_M] * sin_ref[...]
    kr_ref[...] = krope
    kcat_ref[:, :KV_LORA] = ckv.astype(BF16)
    kcat_ref[:, KV_LORA:] = krope.astype(BF16)
    qm_ref[...] = proj[:, _C_M:_C_END]
    for r0 in range(0, tm, rows):
        ya, v = _gmlp_chunk(proj[r0:r0 + rows, _C_UV:_C_Q], gv_ref, bv_ref, ws_ref, bst_ref, goa_ref, rows)
        ya_ref[r0:r0 + rows, :] = ya
        v_ref[r0:r0 + rows, :] = v
    cq = _rms(proj[:, _C_Q:_C_KV], gq_ref[...])
    q = _dot(cq.astype(BF16), wq_ref[...])
    nr = MLA_HEADS * QK_NOPE
    rw = MLA_HEADS * QK_ROPE
    qrope = q[:, nr:nr + rw] * cos8_ref[...] + q[:, nr + rw:] * sin8_ref[...]
    for h in range(MLA_HEADS):
        qn = q[:, h * QK_NOPE:(h + 1) * QK_NOPE].astype(BF16)
        q_ref[h, :, :KV_LORA] = _dot(qn, wuk_ref[h]).astype(BF16)
        q_ref[h, :, KV_LORA:] = qrope[:, h * QK_ROPE:(h + 1) * QK_ROPE].astype(BF16)


def _front(x, wts, cos2, sin2, cos8, sin8, tm, rows):
    n = x.shape[0]
    nper = cos2.shape[0] // tm
    row = lambda w: pl.BlockSpec((tm, w), lambda i: (i, 0))
    tab = lambda w: pl.BlockSpec((tm, w), lambda i: (i % nper, 0))
    consts = [wts[k] for k in ("g_attn", "w_ext", "g_kv")]
    consts_a = [wts[k] for k in ("g_v", "b_v", "w_s", "b_st", "g_out_a")]
    consts_q = [wts[k] for k in ("g_q", "w_uqp", "w_ukt")]
    full = lambda arrs: [_full(a.shape) for a in arrs]
    sds = jax.ShapeDtypeStruct
    return pl.pallas_call(
        functools.partial(_front_kernel, tm=tm, rows=rows),
        grid=(n // tm,),
        in_specs=[row(D_MODEL)] + full(consts) + [tab(QK_ROPE)] * 2 + full(consts_a) + full(consts_q)
        + [tab(MLA_HEADS * QK_ROPE)] * 2,
        out_specs=[row(GMLP_DIM), row(GMLP_DIM), row(KV_LORA), row(QK_ROPE), row(QCAT), row(MEM_DIM),
                   pl.BlockSpec((MLA_HEADS, tm, QCAT), lambda i: (0, i, 0))],
        out_shape=[sds((n, GMLP_DIM), BF16), sds((n, GMLP_DIM), F32), sds((n, KV_LORA), F32), sds((n, QK_ROPE), F32),
                   sds((n, QCAT), BF16), sds((n, MEM_DIM), F32), sds((MLA_HEADS, n, QCAT), BF16)],
        compiler_params=_cparams(1), name="front",
    )(x, *consts, cos2, sin2, *consts_a, *consts_q, cos8, sin8)


def _mla_finish(o, wuv_ref, gob_ref, tq):
    ys = [_dot(o[h * tq:(h + 1) * tq].astype(BF16), wuv_ref[h]) for h in range(MLA_HEADS)]
    return _rms(jnp.concatenate(ys, axis=-1), gob_ref[...]).astype(BF16)


def _mla_attn_kernel(q_ref, k_ref, wuvt_ref, gob_ref, *refs, tq, pack_rows):
    if pack_rows:
        pu_ref, pv_ref, yb_ref, tab_ref, m_ref, l_ref, acc_ref = refs
    else:
        yb_ref, m_ref, l_ref, acc_ref = refs
    i = pl.program_id(1)
    cols = MLA_HEADS * tq
    q = q_ref[...].reshape(cols, QCAT)
    m_ref[...] = jnp.full((1, cols), -jnp.inf, F32)
    l_ref[...] = jnp.zeros((1, cols), F32)
    acc_ref[...] = jnp.zeros((KV_LORA, cols), F32)
    key_chunk = lax.broadcasted_iota(I32, (tq, tq), 0) >> CHUNK_SHIFT
    qry_chunk = lax.broadcasted_iota(I32, (tq, tq), 1) >> CHUNK_SHIFT
    allowed = jnp.concatenate([key_chunk <= qry_chunk] * MLA_HEADS, axis=1)

    def block(j, diagonal):
        k = k_ref[pl.ds(pl.multiple_of(j * tq, tq), tq), :]
        v_t = k[:, :KV_LORA].astype(F32).T.astype(BF16)
        s = _dot_nt(k, q) * MLA_SCALE
        if diagonal:
            s = jnp.where(allowed, s, -jnp.inf)
        m_old = m_ref[...]
        m_new = jnp.maximum(m_old, jnp.max(s, axis=0, keepdims=True))
        alpha = jnp.exp(m_old - m_new)
        p = jnp.exp(s - m_new)
        l_ref[...] = alpha * l_ref[...] + jnp.sum(p, axis=0, keepdims=True)
        acc_ref[...] = alpha * acc_ref[...] + _dot(v_t, p.astype(BF16))
        m_ref[...] = m_new

    def body(j, carry):
        block(j, False)
        return carry

    lax.fori_loop(0, i, body, 0)
    if pack_rows:
        _pack_rows(pu_ref, pv_ref, tab_ref, pack_rows)
    block(i, True)
    o = (acc_ref[...] / l_ref[...]).astype(BF16)
    ys = [_dot(wuvt_ref[h], o[:, h * tq:(h + 1) * tq]) for h in range(MLA_HEADS)]
    y = jnp.concatenate(ys, axis=0)
    y = y * lax.rsqrt(jnp.mean(y * y, axis=0, keepdims=True) + EPS) * gob_ref[...]
    yb_ref[...] = y.T.astype(BF16)


def _mla_attn(qcat, kcat, w_uvtt, g_out_b_col, peer_u, peer_v, batch, seq, tq):
    n = batch * seq
    nq = seq // tq
    ne = peer_u.shape[0]
    pack_rows = ne // (batch * nq) if ne % (batch * nq) == 0 and ne // (batch * nq) <= PACK_ROWS_MAX else 0
    in_specs = [pl.BlockSpec((MLA_HEADS, tq, QCAT), lambda b, i: (0, b * nq + i, 0)),
                pl.BlockSpec((seq, QCAT), lambda b, i: (b, 0)),
                _full(w_uvtt.shape), _full((MLA_DIM, 1))]
    out_specs = [pl.BlockSpec((tq, MLA_DIM), lambda b, i: (b * nq + i, 0))]
    out_shape = [jax.ShapeDtypeStruct((n, MLA_DIM), BF16)]
    args = [qcat, kcat, w_uvtt, g_out_b_col]
    if pack_rows:
        in_specs += [pl.BlockSpec((pack_rows, D_MODEL), lambda b, i: (b * nq + i, 0))] * 2
        out_specs.append(pl.BlockSpec((pack_rows * ROW_CHUNKS, LANES), lambda b, i: (b * nq + i, 0)))
        out_shape.append(jax.ShapeDtypeStruct((ne * ROW_CHUNKS, LANES), WORD))
        args += [peer_u, peer_v]
    outs = pl.pallas_call(
        functools.partial(_mla_attn_kernel, tq=tq, pack_rows=pack_rows),
        grid=(batch, nq),
        in_specs=in_specs, out_specs=out_specs, out_shape=out_shape,
        scratch_shapes=[pltpu.VMEM((1, MLA_HEADS * tq), F32), pltpu.VMEM((1, MLA_HEADS * tq), F32),
                        pltpu.VMEM((KV_LORA, MLA_HEADS * tq), F32)],
        compiler_params=_cparams(2), name="mla_attn",
    )(*args)
    return outs[0], (outs[1] if pack_rows else _peer_pack(peer_u, peer_v))


def _mla_dec_kernel(q_ref, cc_ref, ck_ref, kn_ref, wuv_ref, gob_ref, yb_ref, *, t):
    rows = MLA_HEADS * t
    q = q_ref[...].reshape(rows, QCAT)
    cc = cc_ref[0].astype(BF16)
    ck = ck_ref[0].astype(BF16)
    kn = kn_ref[...]
    s_c = (_dot_nt(q[:, :KV_LORA], cc) + _dot_nt(q[:, KV_LORA:], ck)) * MLA_SCALE
    s_n = _dot_nt(q, kn) * MLA_SCALE
    m = jnp.maximum(jnp.max(s_c, axis=-1, keepdims=True), jnp.max(s_n, axis=-1, keepdims=True))
    p_c = jnp.exp(s_c - m)
    p_n = jnp.exp(s_n - m)
    l = jnp.sum(p_c, axis=-1, keepdims=True) + jnp.sum(p_n, axis=-1, keepdims=True)
    o = (_dot(p_c.astype(BF16), cc) + _dot(p_n.astype(BF16), kn[:, :KV_LORA])) / l
    yb_ref[...] = _mla_finish(o, wuv_ref, gob_ref, t)


def _mla_dec(qcat, cache_ckv, cache_krope, kcat, w_uvt, g_out_b, batch, t):
    past = cache_ckv.shape[1]
    return pl.pallas_call(
        functools.partial(_mla_dec_kernel, t=t),
        grid=(batch,),
        in_specs=[pl.BlockSpec((MLA_HEADS, t, QCAT), lambda b: (0, b, 0)),
                  pl.BlockSpec((1, past, KV_LORA), lambda b: (b, 0, 0)),
                  pl.BlockSpec((1, past, QK_ROPE), lambda b: (b, 0, 0)),
                  pl.BlockSpec((t, QCAT), lambda b: (b, 0)),
                  _full(w_uvt.shape), _full((1, MLA_DIM))],
        out_specs=pl.BlockSpec((t, MLA_DIM), lambda b: (b, 0)),
        out_shape=jax.ShapeDtypeStruct((batch * t, MLA_DIM), BF16),
        compiler_params=_cparams(1), name="mla_dec",
    )(qcat, cache_ckv, cache_krope, kcat, w_uvt, g_out_b)


def _memkv_kernel(mem_ref, g_ref, wk_ref, wv_ref, mk_ref, mv_ref):
    mn = _rms(mem_ref[...], g_ref[...]).astype(BF16)
    mk_ref[...] = _dot(mn, wk_ref[...])
    mv_ref[...] = _dot(mn, wv_ref[...])


def _memkv(mem, g_mem, w_mk, w_mv):
    n = mem.shape[0]
    tm = MEM_TOKENS
    return pl.pallas_call(
        _memkv_kernel,
        grid=(n // tm,),
        in_specs=[pl.BlockSpec((tm, D_MODEL), lambda i: (i, 0)), _full((1, D_MODEL)),
                  _full(w_mk.shape), _full(w_mv.shape)],
        out_specs=[pl.BlockSpec((tm, MEM_DIM), lambda i: (i, 0))] * 2,
        out_shape=[jax.ShapeDtypeStruct((n, MEM_DIM), F32)] * 2,
        compiler_params=_cparams(1), name="mem_kv",
    )(mem, g_mem, w_mk, w_mv)


def _mem_attend(qm, mk, mv, gom_ref):
    outs = []
    for h in range(MEM_HEADS):
        sl = slice(h * MEM_HEAD_DIM, (h + 1) * MEM_HEAD_DIM)
        s = _dot_nt(qm[:, sl].astype(BF16), mk[:, sl].astype(BF16)) * MEM_SCALE
        e = jnp.exp(s - jnp.max(s, axis=-1, keepdims=True))
        p = e / jnp.sum(e, axis=-1, keepdims=True)
        outs.append(_dot(p.astype(BF16), mv[:, sl].astype(BF16)))
    return _rms(jnp.concatenate(outs, axis=-1), gom_ref[...]).astype(BF16)


def _merge_kernel(x_ref, ya_ref, yb_ref, qm_ref, mk_ref, mv_ref, gom_ref, w_ref, gf_ref, x1_ref, xf_ref, *, seqs):
    a0, a1 = GMLP_DIM, GMLP_DIM + MLA_DIM
    t = x_ref.shape[0] // seqs
    ym = jnp.concatenate([_mem_attend(qm_ref[s * t:(s + 1) * t, :], mk_ref[s * MEM_TOKENS:(s + 1) * MEM_TOKENS, :],
                                      mv_ref[s * MEM_TOKENS:(s + 1) * MEM_TOKENS, :], gom_ref)
                          for s in range(seqs)], axis=0)
    y = _dot(ya_ref[...], w_ref[:a0, :]) + _dot(yb_ref[...], w_ref[a0:a1, :]) + _dot(ym, w_ref[a1:, :])
    x1 = x_ref[...] + y
    x1_ref[...] = x1
    xf_ref[...] = _rms(x1, gf_ref[...])


def _merge(x, ya, yb, q_m, mk, mv, g_out_m, w_out, g_ffn, tm, tokens_per_seq):
    n = x.shape[0]
    seqs = max(tm // tokens_per_seq, 1)
    tiles_per_seq = max(tokens_per_seq // tm, 1)
    row = lambda w: pl.BlockSpec((tm, w), lambda i: (i, 0))
    mem = pl.BlockSpec((seqs * MEM_TOKENS, MEM_DIM), lambda i: (i // tiles_per_seq, 0))
    return pl.pallas_call(
        functools.partial(_merge_kernel, seqs=seqs),
        grid=(n // tm,),
        in_specs=[row(D_MODEL), row(GMLP_DIM), row(MLA_DIM), row(MEM_DIM), mem, mem, _full((1, MEM_DIM)),
                  _full(w_out.shape), _full((1, D_MODEL))],
        out_specs=[row(D_MODEL), row(D_MODEL)],
        out_shape=[jax.ShapeDtypeStruct((n, D_MODEL), F32)] * 2,
        compiler_params=_cparams(1), name="merge",
    )(x, ya, yb, q_m, mk, mv, g_out_m, w_out, g_ffn)


def _top16(s, order, payload):
    vals, picks = [], []
    for _ in range(PEER_TOPK):
        m = jnp.max(s, axis=0, keepdims=True)
        first = jnp.min(jnp.where(s == m, order, BIG_ORDER), axis=0, keepdims=True)
        sel = order == first
        vals.append(m)
        picks.append(first if payload is None else jnp.max(jnp.where(sel, payload, -1.0), axis=0, keepdims=True))
        s = jnp.where(sel, -jnp.inf, s)
    return jnp.concatenate(vals, axis=0), jnp.concatenate(picks, axis=0)


def _pair_candidates(v1, i1, v2, i2):
    tm = v1.shape[1]
    row = lax.broadcasted_iota(I32, (PEER_TOPK, tm), 0)
    top = row < 8
    low = (row & 7).astype(F32)
    rowf = row.astype(F32)
    halves = lambda x: jnp.where(top, x, pltpu.roll(x, 8, 0))
    v1h, i1h, v2h, i2h = halves(v1), halves(i1), halves(v2), halves(i2)
    cand = [v1[0:1] + v2]
    eid = [i1[0:1] * N_KEYS + i2]
    order = [rowf]
    for a0, a1 in ((1, 2), (3, 4)):
        cand.append(jnp.where(top, v1[a0:a0 + 1], v1[a1:a1 + 1]) + v2h)
        eid.append(jnp.where(top, i1[a0:a0 + 1], i1[a1:a1 + 1]) * N_KEYS + i2h)
        order.append(jnp.where(top, a0 * PEER_TOPK, a1 * PEER_TOPK) + low)
    listed = rowf < PAIR_ROWS_LISTED
    cand.append(jnp.where(listed, -jnp.inf, v1 + v2[0:1]))
    eid.append(i1 * N_KEYS + i2[0:1])
    order.append(jnp.where(listed, BIG_ORDER, rowf * PEER_TOPK))
    listed = low < PAIR_ROWS_LISTED
    cand.append(jnp.where(listed, -jnp.inf, v1h + jnp.where(top, v2[1:2], v2[2:3])))
    eid.append(i1h * N_KEYS + jnp.where(top, i2[1:2], i2[2:3]))
    order.append(jnp.where(listed, BIG_ORDER, low * PEER_TOPK + jnp.where(top, 1.0, 2.0)))
    return jnp.concatenate(cand, axis=0), jnp.concatenate(order, axis=0), jnp.concatenate(eid, axis=0)


def _peer_topk_kernel(xf_ref, wpq_ref, sk1_ref, sk2_ref, eid_ref, gate_ref, *, tm):
    q = _dot(xf_ref[...].astype(BF16), wpq_ref[...])
    key_order = lax.broadcasted_iota(I32, (N_KEYS, tm), 0).astype(F32)
    for h in range(PEER_HEADS):
        qa = q[:, h * PEER_QDIM:h * PEER_QDIM + PEER_HALF].astype(BF16)
        qb = q[:, h * PEER_QDIM + PEER_HALF:(h + 1) * PEER_QDIM].astype(BF16)
        v1, i1 = _top16(_dot_nt(sk1_ref[...], qa), key_order, None)
        v2, i2 = _top16(_dot_nt(sk2_ref[...], qb), key_order, None)
        vals, eid = _top16(*_pair_candidates(v1, i1, v2, i2))
        e = jnp.exp(vals - vals[0:1])
        rows = slice(h * PEER_TOPK, (h + 1) * PEER_TOPK)
        gate_ref[rows, :] = e / jnp.sum(e, axis=0, keepdims=True)
        eid_ref[rows, :] = eid.astype(I32)


def _peer_topk(xf, w_pq, sk1, sk2, tm):
    n = xf.shape[0]
    out = pl.BlockSpec((PEER_PAIRS, tm), lambda i: (0, i))
    return pl.pallas_call(
        functools.partial(_peer_topk_kernel, tm=tm),
        grid=(n // tm,),
        in_specs=[pl.BlockSpec((tm, D_MODEL), lambda i: (i, 0)), _full(w_pq.shape),
                  _full(sk1.shape), _full(sk2.shape)],
        out_specs=[out, out],
        out_shape=[jax.ShapeDtypeStruct((PEER_PAIRS, n), I32), jax.ShapeDtypeStruct((PEER_PAIRS, n), F32)],
        compiler_params=_cparams(1), name="peer_topk",
    )(xf, w_pq, sk1, sk2)


def _peer_pack_kernel(pu_ref, pv_ref, tab_ref, *, tr):
    _pack_rows(pu_ref, pv_ref, tab_ref, tr)


def _peer_pack(peer_u, peer_v):
    ne = peer_u.shape[0]
    tr = 256
    return pl.pallas_call(
        functools.partial(_peer_pack_kernel, tr=tr),
        grid=(ne // tr,),
        in_specs=[pl.BlockSpec((tr, D_MODEL), lambda i: (i, 0))] * 2,
        out_specs=pl.BlockSpec((tr * ROW_CHUNKS, LANES), lambda i: (i, 0)),
        out_shape=jax.ShapeDtypeStruct((ne * ROW_CHUNKS, LANES), WORD),
        compiler_params=_cparams(1), name="peer_pack",
    )(peer_u, peer_v)


def _unpack(words):
    return tuple(pltpu.unpack_elementwise(words, index=i, packed_dtype=BF16, unpacked_dtype=F32) for i in (0, 1))


def _peer_mix_kernel(ids_ref, gate_ref, xf_ref, x1_ref, gfin_ref, tab_ref, y_ref, *scratch, tt):
    bufs, sems = scratch[:MIX_SLOTS], scratch[MIX_SLOTS]
    half_chunks = ROW_CHUNKS // 2
    step, last = pl.program_id(0), pl.num_programs(0) - 1

    def issue(t, k):
        for p in range(PEER_PAIRS):
            row0 = pl.multiple_of(ids_ref[0, t, p] * ROW_CHUNKS, ROW_CHUNKS)
            pltpu.make_async_copy(tab_ref.at[pl.ds(row0, ROW_CHUNKS)],
                                  bufs[k].at[pl.ds(p * ROW_PITCH, ROW_CHUNKS)],
                                  sems.at[k]).start(priority=p % 2)

    def wait(k):
        n = PEER_PAIRS * ROW_CHUNKS
        pltpu.make_async_copy(tab_ref.at[pl.ds(0, n)], bufs[k].at[pl.ds(0, n)], sems.at[k]).wait()

    def words(k, c):
        return bufs[k][pl.ds(c, PEER_PAIRS, stride=ROW_PITCH), :]

    pair_of_lane = lax.broadcasted_iota(I32, (PEER_PAIRS, 2 * PEER_PAIRS), 1) >> 1
    own_lane = pair_of_lane == lax.broadcasted_iota(I32, (PEER_PAIRS, 2 * PEER_PAIRS), 0)
    sub = lax.broadcasted_iota(I32, (8, 2 * PEER_PAIRS), 0)
    lane = lax.broadcasted_iota(I32, (8, 2 * PEER_PAIRS), 1)
    keep = (sub < 4) & ((lane & 1) == (sub >> 1))
    high_part = (sub & 1) == 0

    def gate_weights(k, xrow, grow):
        acc = jnp.zeros((PEER_PAIRS, LANES), F32)
        for c in range(half_chunks):
            lo, hi = _unpack(words(k, c))
            acc = acc + lo * xrow[:, c * LANES:(c + 1) * LANES]
            acc = acc + hi * xrow[:, HALF_D + c * LANES:HALF_D + (c + 1) * LANES]
        h = jnp.sum(acc, axis=-1, keepdims=True)
        hrow = jnp.sum(jnp.where(own_lane, h, 0.0), axis=0, keepdims=True)
        w = grow * _gelu(hrow)
        w_hi = w.astype(BF16).astype(F32)
        return jnp.where(keep, jnp.where(high_part, w_hi, w - w_hi), 0.0).astype(BF16)

    def mix_values(k, lhs):
        rhs = jnp.concatenate([pltpu.bitcast(words(k, half_chunks + c), BF16) for c in range(half_chunks)], axis=-1)
        o = _dot(lhs, rhs)
        return jnp.concatenate([o[0:1] + o[1:2], o[2:3] + o[3:4]], axis=-1)

    @pl.when(step == 0)
    def _():
        for k in range(MIX_AHEAD):
            issue(k, k)

    def group(g, carry):
        rows = pl.ds(pl.multiple_of(g * MIX_SLOTS, MIX_SLOTS), MIX_SLOTS)
        x_tile, g_tile = xf_ref[rows, :], gate_ref[rows, :]
        lhs_prev, yrows = None, []
        for k in range(MIX_SLOTS):
            wait(k)
            issue(g * MIX_SLOTS + k + MIX_AHEAD, (k + MIX_AHEAD) % MIX_SLOTS)
            lhs = gate_weights(k, x_tile[k:k + 1], g_tile[k:k + 1])
            if lhs_prev is not None:
                yrows.append(mix_values(k - 1, lhs_prev))
            lhs_prev = lhs
        yrows.append(mix_values(MIX_SLOTS - 1, lhs_prev))
        y_ref[rows, :] = _rms(x1_ref[rows, :] + jnp.concatenate(yrows, axis=0), gfin_ref[...])
        return carry

    lax.fori_loop(0, tt // MIX_SLOTS, group, 0)

    @pl.when(step == last)
    def _():
        for k in range(MIX_AHEAD):
            wait(k)


def _peer_mix(eid_t, gate_t, xf, x1, g_final, table, tt):
    n = xf.shape[0]
    steps = n // tt
    eid = eid_t.T
    look = jnp.concatenate([eid[tt:], eid[-tt:]], axis=0).reshape(steps, tt, PEER_PAIRS)[:, :MIX_SLOTS]
    ids = jnp.concatenate([eid.reshape(steps, tt, PEER_PAIRS), look], axis=1)
    gate = jnp.repeat(gate_t.T, 2, axis=1)
    row = pl.BlockSpec((tt, D_MODEL), lambda i: (i, 0))
    assert MIX_SLOTS == 8 and tt % MIX_SLOTS == 0
    return pl.pallas_call(
        functools.partial(_peer_mix_kernel, tt=tt),
        grid=(steps,),
        in_specs=[pl.BlockSpec((1, tt + MIX_SLOTS, PEER_PAIRS), lambda i: (i, 0, 0), memory_space=pltpu.SMEM),
                  pl.BlockSpec((tt, 2 * PEER_PAIRS), lambda i: (i, 0)),
                  row, row, _full((1, D_MODEL)),
                  pl.BlockSpec(memory_space=pl.ANY)],
        out_specs=row,
        out_shape=jax.ShapeDtypeStruct((n, D_MODEL), F32),
        scratch_shapes=[pltpu.VMEM((PEER_PAIRS * ROW_PITCH, LANES), WORD)] * MIX_SLOTS
        + [pltpu.SemaphoreType.DMA((MIX_SLOTS,))],
        compiler_params=_cparams(1), name="peer_mix",
    )(ids, gate, xf, x1, g_final, table)


def _rope_tables(pos):
    half = QK_ROPE // 2
    inv = ROPE_THETA ** (-jnp.arange(half, dtype=F32) / half)
    ang = pos.astype(F32)[:, None] * inv[None, :]
    cos, sin = jnp.cos(ang), jnp.sin(ang)
    return jnp.concatenate([cos, cos], -1), jnp.concatenate([sin, sin], -1)


def _rot_cols(w):
    half = w.shape[-1] // 2
    return jnp.concatenate([-w[..., half:], w[..., :half]], axis=-1)


def _group(x, tokens_per_seq, pos, cache, mem_kv, wts, g_final):
    n = x.shape[0]
    nseq = n // tokens_per_seq
    tm = min(256, n)
    cos2, sin2 = _rope_tables(pos)
    reps = max(tm // tokens_per_seq, 1)
    cos2, sin2 = jnp.tile(cos2, (reps, 1)), jnp.tile(sin2, (reps, 1))
    cos8, sin8 = jnp.tile(cos2, (1, MLA_HEADS)), jnp.tile(sin2, (1, MLA_HEADS))

    ya, v, ckv, krope, kcat, q_m, qcat = _front(x, wts, cos2, sin2, cos8, sin8, tm, min(GMLP_CHUNK, tokens_per_seq))
    if cache is None:
        yb, table = _mla_attn(qcat, kcat, jnp.swapaxes(wts["w_uvt"], 1, 2), wts["g_out_b"].reshape(MLA_DIM, 1),
                              wts["peer_u"], wts["peer_v"], nseq, tokens_per_seq, min(256, tokens_per_seq))
    else:
        yb = _mla_dec(qcat, cache[0], cache[1], kcat, wts["w_uvt"], wts["g_out_b"], nseq, tokens_per_seq)
        table = wts["table"]
    mk, mv = mem_kv
    x1, xf = _merge(x, ya, yb, q_m, mk, mv, wts["g_out_m"], wts["w_out"], wts["g_ffn"], tm, tokens_per_seq)
    eid_t, gate_t = _peer_topk(xf, wts["w_pq"], wts["sk1"], wts["sk2"], tm)
    y = _peer_mix(eid_t, gate_t, xf, x1, g_final, table, min(64, n))
    return y, ckv, krope, v, table


def kernel(x_prompt, x_sample, cache_mla_ckv, cache_mla_krope, cache_mem_k, cache_mem_v, mem_prompt, g_attn, w_in, g_v, b_v, w_s, b_s, g_q, w_uq, w_uk, w_uv, g_kv, g_mem, w_mk, w_mv, g_out_a, g_out_b, g_out_m, w_out, g_ffn, w_pq, sub_keys1, sub_keys2, peer_u, peer_v, g_final):
    assert w_in.shape[0] == 1, "the final norm is fused after the single layer"
    l = 0
    bp, sp, _ = x_prompt.shape
    bs, ts, _ = x_sample.shape
    past = cache_mla_ckv.shape[2]
    gfin = g_final.reshape(1, D_MODEL)
    wi = w_in[l]
    w_ext = jnp.concatenate([wi[:, :_C_KROT], _rot_cols(wi[:, _C_KR:_C_KROT]), wi[:, _C_KROT:]], axis=1)
    wq = w_uq[l].reshape(Q_LORA, MLA_HEADS, QK_NOPE + QK_ROPE)
    wq_rope = wq[:, :, QK_NOPE:]
    w_uqp = jnp.concatenate([wq[:, :, :QK_NOPE].reshape(Q_LORA, -1), wq_rope.reshape(Q_LORA, -1),
                             _rot_cols(wq_rope).reshape(Q_LORA, -1)], axis=1)
    wts = {
        "g_attn": g_attn[l].reshape(1, -1), "w_ext": w_ext.astype(BF16), "g_kv": g_kv[l].reshape(1, -1),
        "g_v": g_v[l].reshape(1, -1), "b_v": b_v[l].reshape(1, -1), "w_s": w_s[l], "b_st": b_s[l].T,
        "g_out_a": g_out_a[l].reshape(1, -1), "g_q": g_q[l].reshape(1, -1), "w_uqp": w_uqp.astype(BF16),
        "w_ukt": jnp.transpose(w_uk[l], (1, 2, 0)).astype(BF16),
        "w_uvt": jnp.transpose(w_uv[l], (1, 0, 2)).astype(BF16),
        "g_out_b": g_out_b[l].reshape(1, -1), "g_out_m": g_out_m[l].reshape(1, -1),
        "w_out": w_out[l].astype(BF16), "g_ffn": g_ffn[l].reshape(1, -1), "w_pq": w_pq[l].astype(BF16),
        "sk1": sub_keys1[l].astype(BF16), "sk2": sub_keys2[l].astype(BF16),
        "peer_u": peer_u[l], "peer_v": peer_v[l],
    }
    mk, mv = _memkv(mem_prompt.reshape(bp * MEM_TOKENS, D_MODEL), g_mem[l].reshape(1, -1),
                    w_mk[l].astype(BF16), w_mv[l].astype(BF16))
    yp, ckv_p, kr_p, _, wts["table"] = _group(x_prompt.reshape(bp * sp, D_MODEL), sp, jnp.arange(sp), None, (mk, mv),
                                              wts, gfin)
    mem_s = (cache_mem_k[l].reshape(bs * MEM_TOKENS, MEM_DIM), cache_mem_v[l].reshape(bs * MEM_TOKENS, MEM_DIM))
    ys, ckv_s, kr_s, gv_s, _ = _group(x_sample.reshape(bs * ts, D_MODEL), ts, past + jnp.arange(ts),
                                      (cache_mla_ckv[l], cache_mla_krope[l]), mem_s, wts, gfin)
    return (yp.reshape(bp, sp, D_MODEL), ys.reshape(bs, ts, D_MODEL),
            ckv_p.reshape(1, bp, sp, KV_LORA), kr_p.reshape(1, bp, sp, QK_ROPE),
            mk.reshape(1, bp, MEM_TOKENS, MEM_HEADS, MEM_HEAD_DIM),
            mv.reshape(1, bp, MEM_TOKENS, MEM_HEADS, MEM_HEAD_DIM),
            ckv_s.reshape(1, bs, ts, KV_LORA), kr_s.reshape(1, bs, ts, QK_ROPE),
            gv_s.reshape(1, bs, ts, GMLP_DIM))
```

```python
import functools
import math

import jax
import jax.numpy as jnp
from jax import lax
from jax.experimental import pallas as pl
from jax.experimental.pallas import tpu as pltpu

F32 = jnp.float32
BF16 = jnp.bfloat16
I32 = jnp.int32
WORD = jnp.uint32

D_MODEL = 2048
CHUNK = 64
CHUNK_SHIFT = 6
EPS = 1e-6
GMLP_CHUNK = 128
GMLP_DIM = 512
GMLP_GROUPS = 4
GMLP_GROUP_DIM = 128
V_HEAD = 128
QK_NOPE = 128
QK_ROPE = 64
MLA_HEADS = 8
MLA_DIM = 1024
Q_LORA = 512
KV_LORA = 256
ROPE_THETA = 10000.0
MLA_SCALE = (QK_NOPE + QK_ROPE) ** -0.5
QCAT = KV_LORA + QK_ROPE
MEM_TOKENS = 256
MEM_HEADS = 4
MEM_DIM = 512
MEM_HEAD_DIM = 128
MEM_SCALE = MEM_HEAD_DIM ** -0.5
PEER_HEADS = 8
N_KEYS = 128
PEER_QDIM = 256
PEER_HALF = 128
PEER_TOPK = 16
PEER_PAIRS = PEER_HEADS * PEER_TOPK
PAIR_ROWS_LISTED = 5
BIG_ORDER = 1e9
HALF_D = D_MODEL // 2
LANES = 128
ROW_CHUNKS = 2 * HALF_D // LANES
ROW_PITCH = 17
MERGE_ROWS = 512
PACK_ROWS_MAX = 512
MIX_SLOTS = 8
MIX_AHEAD = 6

_C_UV, _C_Q, _C_KV, _C_KR, _C_KROT, _C_M, _C_END = 0, 1024, 1536, 1792, 1856, 1920, 2432

VMEM_LIMIT = 48 * 1024 * 1024


def _cparams(n_grid):
    return pltpu.CompilerParams(dimension_semantics=("arbitrary",) * n_grid,
                                vmem_limit_bytes=VMEM_LIMIT)


def _rms(x, g):
    return x * lax.rsqrt(jnp.mean(x * x, axis=-1, keepdims=True) + EPS) * g


def _gelu(x):
    return x * (0.5 * (1.0 + jnp.tanh(math.sqrt(2.0 / math.pi) * (x + 0.044715 * (x * x * x)))))


def _dot(a, b):
    return jnp.dot(a, b, preferred_element_type=F32)


def _dot_nt(a, b):
    return lax.dot_general(a, b, (((1,), (1,)), ((), ())), preferred_element_type=F32)


def _full(shape):
    n = len(shape)
    return pl.BlockSpec(shape, lambda *_: (0,) * n)


def _pack_rows(pu_ref, pv_ref, tab_ref, tr):
    for src, base in ((pu_ref, 0), (pv_ref, ROW_CHUNKS // 2)):
        x = src[...]
        words = pltpu.bitcast(pltpu.pack_elementwise([x[:, :HALF_D], x[:, HALF_D:]], packed_dtype=BF16), WORD)
        for c in range(ROW_CHUNKS // 2):
            tab_ref[pl.ds(base + c, tr, stride=ROW_CHUNKS), :] = words[:, c * LANES:(c + 1) * LANES]


def _gmlp_chunk(uv, gv_ref, bv_ref, ws_ref, bst_ref, goa_ref, rows):
    uv = _gelu(uv)
    u = uv[:, :GMLP_DIM]
    vr = uv[:, GMLP_DIM:]
    mu = jnp.mean(vr, axis=-1, keepdims=True)
    var = jnp.mean(jnp.square(vr - mu), axis=-1, keepdims=True)
    v = (vr - mu) * lax.rsqrt(var + EPS) * gv_ref[...] + bv_ref[...]
    r = lax.broadcasted_iota(I32, (rows, rows), 0)
    c = lax.broadcasted_iota(I32, (rows, rows), 1)
    zs = []
    for g in range(GMLP_GROUPS):
        w = jnp.where(r >= c, ws_ref[g, :rows, :rows], 0.0).astype(BF16)
        vg = v[:, g * GMLP_GROUP_DIM:(g + 1) * GMLP_GROUP_DIM].astype(BF16)
        zs.append(_dot(w, vg) + bst_ref[:rows, g:g + 1])
    y = u * jnp.concatenate(zs, axis=-1)
    return _rms(y, goa_ref[...]).astype(BF16), v


def _front_kernel(x_ref, g_ref, w_ref, gkv_ref, cos_ref, sin_ref, gv_ref, bv_ref, ws_ref, bst_ref, goa_ref,
                  gq_ref, wq_ref, wuk_ref, cos8_ref, sin8_ref,
                  ya_ref, v_ref, ckv_ref, kr_ref, kcat_ref, qm_ref, q_ref, *, tm, rows):
    xn = _rms(x_ref[...], g_ref[...])
    proj = _dot(xn.astype(BF16), w_ref[...])
    ckv = _rms(proj[:, _C_KV:_C_KR], gkv_ref[...])
    ckv_ref[...] = ckv
    krope = proj[:, _C_KR:_C_KROT] * cos_ref[...] + proj[:, _C_KROT:_C_M] * sin_ref[...]
    kr_ref[...] = krope
    kcat_ref[:, :KV_LORA] = ckv.astype(BF16)
    kcat_ref[:, KV_LORA:] = krope.astype(BF16)
    qm_ref[...] = proj[:, _C_M:_C_END]
    for r0 in range(0, tm, rows):
        ya, v = _gmlp_chunk(proj[r0:r0 + rows, _C_UV:_C_Q], gv_ref, bv_ref, ws_ref, bst_ref, goa_ref, rows)
        ya_ref[r0:r0 + rows, :] = ya
        v_ref[r0:r0 + rows, :] = v
    cq = _rms(proj[:, _C_Q:_C_KV], gq_ref[...])
    q = _dot(cq.astype(BF16), wq_ref[...])
    nr = MLA_HEADS * QK_NOPE
    rw = MLA_HEADS * QK_ROPE
    qrope = q[:, nr:nr + rw] * cos8_ref[...] + q[:, nr + rw:] * sin8_ref[...]
    for h in range(MLA_HEADS):
        qn = q[:, h * QK_NOPE:(h + 1) * QK_NOPE].astype(BF16)
        q_ref[h, :, :KV_LORA] = _dot(qn, wuk_ref[h]).astype(BF16)
        q_ref[h, :, KV_LORA:] = qrope[:, h * QK_ROPE:(h + 1) * QK_ROPE].astype(BF16)


def _front(x, wts, cos2, sin2, cos8, sin8, tm, rows):
    n = x.shape[0]
    nper = cos2.shape[0] // tm
    row = lambda w: pl.BlockSpec((tm, w), lambda i: (i, 0))
    tab = lambda w: pl.BlockSpec((tm, w), lambda i: (i % nper, 0))
    consts = [wts[k] for k in ("g_attn", "w_ext", "g_kv")]
    consts_a = [wts[k] for k in ("g_v", "b_v", "w_s", "b_st", "g_out_a")]
    consts_q = [wts[k] for k in ("g_q", "w_uqp", "w_ukt")]
    full = lambda arrs: [_full(a.shape) for a in arrs]
    sds = jax.ShapeDtypeStruct
    return pl.pallas_call(
        functools.partial(_front_kernel, tm=tm, rows=rows),
        grid=(n // tm,),
        in_specs=[row(D_MODEL)] + full(consts) + [tab(QK_ROPE)] * 2 + full(consts_a) + full(consts_q)
        + [tab(MLA_HEADS * QK_ROPE)] * 2,
        out_specs=[row(GMLP_DIM), row(GMLP_DIM), row(KV_LORA), row(QK_ROPE), row(QCAT), row(MEM_DIM),
                   pl.BlockSpec((MLA_HEADS, tm, QCAT), lambda i: (0, i, 0))],
        out_shape=[sds((n, GMLP_DIM), BF16), sds((n, GMLP_DIM), F32), sds((n, KV_LORA), F32), sds((n, QK_ROPE), F32),
                   sds((n, QCAT), BF16), sds((n, MEM_DIM), F32), sds((MLA_HEADS, n, QCAT), BF16)],
        compiler_params=_cparams(1), name="front",
    )(x, *consts, cos2, sin2, *consts_a, *consts_q, cos8, sin8)


def _mla_finish(o, wuv_ref, gob_ref, tq):
    ys = [_dot(o[h * tq:(h + 1) * tq].astype(BF16), wuv_ref[h]) for h in range(MLA_HEADS)]
    return _rms(jnp.concatenate(ys, axis=-1), gob_ref[...]).astype(BF16)


def _mla_attn_kernel(q_ref, k_ref, wuvt_ref, gob_ref, *refs, tq, pack_rows):
    if pack_rows:
        pu_ref, pv_ref, yb_ref, tab_ref, m_ref, l_ref, acc_ref = refs
    else:
        yb_ref, m_ref, l_ref, acc_ref = refs
    i = pl.program_id(1)
    cols = MLA_HEADS * tq
    q = q_ref[...].reshape(cols, QCAT)
    m_ref[...] = jnp.full((1, cols), -jnp.inf, F32)
    l_ref[...] = jnp.zeros((1, cols), F32)
    acc_ref[...] = jnp.zeros((KV_LORA, cols), F32)
    key_chunk = lax.broadcasted_iota(I32, (tq, tq), 0) >> CHUNK_SHIFT
    qry_chunk = lax.broadcasted_iota(I32, (tq, tq), 1) >> CHUNK_SHIFT
    allowed = jnp.concatenate([key_chunk <= qry_chunk] * MLA_HEADS, axis=1)

    def block(j, diagonal):
        k = k_ref[pl.ds(pl.multiple_of(j * tq, tq), tq), :]
        v_t = k[:, :KV_LORA].astype(F32).T.astype(BF16)
        s = _dot_nt(k, q) * MLA_SCALE
        if diagonal:
            s = jnp.where(allowed, s, -jnp.inf)
        m_old = m_ref[...]
        m_new = jnp.maximum(m_old, jnp.max(s, axis=0, keepdims=True))
        alpha = jnp.exp(m_old - m_new)
        p = jnp.exp(s - m_new)
        l_ref[...] = alpha * l_ref[...] + jnp.sum(p, axis=0, keepdims=True)
        acc_ref[...] = alpha * acc_ref[...] + _dot(v_t, p.astype(BF16))
        m_ref[...] = m_new

    def body(j, carry):
        block(j, False)
        return carry

    lax.fori_loop(0, i, body, 0)
    if pack_rows:
        _pack_rows(pu_ref, pv_ref, tab_ref, pack_rows)
    block(i, True)
    o = (acc_ref[...] / l_ref[...]).astype(BF16)
    ys = [_dot(wuvt_ref[h], o[:, h * tq:(h + 1) * tq]) for h in range(MLA_HEADS)]
    y = jnp.concatenate(ys, axis=0)
    y = y * lax.rsqrt(jnp.mean(y * y, axis=0, keepdims=True) + EPS) * gob_ref[...]
    yb_ref[...] = y.T.astype(BF16)


def _mla_attn(qcat, kcat, w_uvtt, g_out_b_col, peer_u, peer_v, batch, seq, tq):
    n = batch * seq
    nq = seq // tq
    ne = peer_u.shape[0]
    pack_rows = ne // (batch * nq) if ne % (batch * nq) == 0 and ne // (batch * nq) <= PACK_ROWS_MAX else 0
    in_specs = [pl.BlockSpec((MLA_HEADS, tq, QCAT), lambda b, i: (0, b * nq + i, 0)),
                pl.BlockSpec((seq, QCAT), lambda b, i: (b, 0)),
                _full(w_uvtt.shape), _full((MLA_DIM, 1))]
    out_specs = [pl.BlockSpec((tq, MLA_DIM), lambda b, i: (b * nq + i, 0))]
    out_shape = [jax.ShapeDtypeStruct((n, MLA_DIM), BF16)]
    args = [qcat, kcat, w_uvtt, g_out_b_col]
    if pack_rows:
        in_specs += [pl.BlockSpec((pack_rows, D_MODEL), lambda b, i: (b * nq + i, 0))] * 2
        out_specs.append(pl.BlockSpec((pack_rows * ROW_CHUNKS, LANES), lambda b, i: (b * nq + i, 0)))
        out_shape.append(jax.ShapeDtypeStruct((ne * ROW_CHUNKS, LANES), WORD))
        args += [peer_u, peer_v]
    outs = pl.pallas_call(
        functools.partial(_mla_attn_kernel, tq=tq, pack_rows=pack_rows),
        grid=(batch, nq),
        in_specs=in_specs, out_specs=out_specs, out_shape=out_shape,
        scratch_shapes=[pltpu.VMEM((1, MLA_HEADS * tq), F32), pltpu.VMEM((1, MLA_HEADS * tq), F32),
                        pltpu.VMEM((KV_LORA, MLA_HEADS * tq), F32)],
        compiler_params=_cparams(2), name="mla_attn",
    )(*args)
    return outs[0], (outs[1] if pack_rows else _peer_pack(peer_u, peer_v))


def _mla_dec_kernel(q_ref, cc_ref, ck_ref, kn_ref, wuv_ref, gob_ref, yb_ref, *, t):
    rows = MLA_HEADS * t
    q = q_ref[...].reshape(rows, QCAT)
    cc = cc_ref[0].astype(BF16)
    ck = ck_ref[0].astype(BF16)
    kn = kn_ref[...]
    s_c = (_dot_nt(q[:, :KV_LORA], cc) + _dot_nt(q[:, KV_LORA:], ck)) * MLA_SCALE
    s_n = _dot_nt(q, kn) * MLA_SCALE
    m = jnp.maximum(jnp.max(s_c, axis=-1, keepdims=True), jnp.max(s_n, axis=-1, keepdims=True))
    p_c = jnp.exp(s_c - m)
    p_n = jnp.exp(s_n - m)
    l = jnp.sum(p_c, axis=-1, keepdims=True) + jnp.sum(p_n, axis=-1, keepdims=True)
    o = (_dot(p_c.astype(BF16), cc) + _dot(p_n.astype(BF16), kn[:, :KV_LORA])) / l
    yb_ref[...] = _mla_finish(o, wuv_ref, gob_ref, t)


def _mla_dec(qcat, cache_ckv, cache_krope, kcat, w_uvt, g_out_b, batch, t):
    past = cache_ckv.shape[1]
    return pl.pallas_call(
        functools.partial(_mla_dec_kernel, t=t),
        grid=(batch,),
        in_specs=[pl.BlockSpec((MLA_HEADS, t, QCAT), lambda b: (0, b, 0)),
                  pl.BlockSpec((1, past, KV_LORA), lambda b: (b, 0, 0)),
                  pl.BlockSpec((1, past, QK_ROPE), lambda b: (b, 0, 0)),
                  pl.BlockSpec((t, QCAT), lambda b: (b, 0)),
                  _full(w_uvt.shape), _full((1, MLA_DIM))],
        out_specs=pl.BlockSpec((t, MLA_DIM), lambda b: (b, 0)),
        out_shape=jax.ShapeDtypeStruct((batch * t, MLA_DIM), BF16),
        compiler_params=_cparams(1), name="mla_dec",
    )(qcat, cache_ckv, cache_krope, kcat, w_uvt, g_out_b)


def _memkv_kernel(mem_ref, g_ref, wk_ref, wv_ref, mk_ref, mv_ref):
    mn = _rms(mem_ref[...], g_ref[...]).astype(BF16)
    mk_ref[...] = _dot(mn, wk_ref[...])
    mv_ref[...] = _dot(mn, wv_ref[...])


def _memkv(mem, g_mem, w_mk, w_mv):
    n = mem.shape[0]
    tm = MEM_TOKENS
    return pl.pallas_call(
        _memkv_kernel,
        grid=(n // tm,),
        in_specs=[pl.BlockSpec((tm, D_MODEL), lambda i: (i, 0)), _full((1, D_MODEL)),
                  _full(w_mk.shape), _full(w_mv.shape)],
        out_specs=[pl.BlockSpec((tm, MEM_DIM), lambda i: (i, 0))] * 2,
        out_shape=[jax.ShapeDtypeStruct((n, MEM_DIM), F32)] * 2,
        compiler_params=_cparams(1), name="mem_kv",
    )(mem, g_mem, w_mk, w_mv)


def _mem_attend(qm, mk, mv, gom_ref):
    outs = []
    for h in range(MEM_HEADS):
        sl = slice(h * MEM_HEAD_DIM, (h + 1) * MEM_HEAD_DIM)
        s = _dot_nt(qm[:, sl].astype(BF16), mk[:, sl].astype(BF16)) * MEM_SCALE
        e = jnp.exp(s - jnp.max(s, axis=-1, keepdims=True))
        p = e / jnp.sum(e, axis=-1, keepdims=True)
        outs.append(_dot(p.astype(BF16), mv[:, sl].astype(BF16)))
    return _rms(jnp.concatenate(outs, axis=-1), gom_ref[...]).astype(BF16)


def _merge_kernel(x_ref, ya_ref, yb_ref, qm_ref, mk_ref, mv_ref, gom_ref, w_ref, gf_ref, x1_ref, xf_ref, *, seqs):
    a0, a1 = GMLP_DIM, GMLP_DIM + MLA_DIM
    t = x_ref.shape[0] // seqs
    ym = jnp.concatenate([_mem_attend(qm_ref[s * t:(s + 1) * t, :], mk_ref[s * MEM_TOKENS:(s + 1) * MEM_TOKENS, :],
                                      mv_ref[s * MEM_TOKENS:(s + 1) * MEM_TOKENS, :], gom_ref)
                          for s in range(seqs)], axis=0)
    y = _dot(ya_ref[...], w_ref[:a0, :]) + _dot(yb_ref[...], w_ref[a0:a1, :]) + _dot(ym, w_ref[a1:, :])
    x1 = x_ref[...] + y
    x1_ref[...] = x1
    xf_ref[...] = _rms(x1, gf_ref[...])


def _merge(x, ya, yb, q_m, mk, mv, g_out_m, w_out, g_ffn, tm, tokens_per_seq):
    n = x.shape[0]
    seqs = max(tm // tokens_per_seq, 1)
    tiles_per_seq = max(tokens_per_seq // tm, 1)
    row = lambda w: pl.BlockSpec((tm, w), lambda i: (i, 0))
    mem = pl.BlockSpec((seqs * MEM_TOKENS, MEM_DIM), lambda i: (i // tiles_per_seq, 0))
    return pl.pallas_call(
        functools.partial(_merge_kernel, seqs=seqs),
        grid=(n // tm,),
        in_specs=[row(D_MODEL), row(GMLP_DIM), row(MLA_DIM), row(MEM_DIM), mem, mem, _full((1, MEM_DIM)),
                  pl.BlockSpec(w_out.shape, lambda i: (0, 0), pipeline_mode=pl.Buffered(1)), _full((1, D_MODEL))],
        out_specs=[row(D_MODEL), row(D_MODEL)],
        out_shape=[jax.ShapeDtypeStruct((n, D_MODEL), F32)] * 2,
        compiler_params=_cparams(1), name="merge",
    )(x, ya, yb, q_m, mk, mv, g_out_m, w_out, g_ffn)


def _top16(s, order, payload):
    vals, picks = [], []
    for _ in range(PEER_TOPK):
        m = jnp.max(s, axis=0, keepdims=True)
        first = jnp.min(jnp.where(s == m, order, BIG_ORDER), axis=0, keepdims=True)
        sel = order == first
        vals.append(m)
        picks.append(first if payload is None else jnp.max(jnp.where(sel, payload, -1.0), axis=0, keepdims=True))
        s = jnp.where(sel, -jnp.inf, s)
    return jnp.concatenate(vals, axis=0), jnp.concatenate(picks, axis=0)


def _pair_candidates(v1, i1, v2, i2):
    tm = v1.shape[1]
    row = lax.broadcasted_iota(I32, (PEER_TOPK, tm), 0)
    top = row < 8
    low = (row & 7).astype(F32)
    rowf = row.astype(F32)
    halves = lambda x: jnp.where(top, x, pltpu.roll(x, 8, 0))
    v1h, i1h, v2h, i2h = halves(v1), halves(i1), halves(v2), halves(i2)
    cand = [v1[0:1] + v2]
    eid = [i1[0:1] * N_KEYS + i2]
    order = [rowf]
    for a0, a1 in ((1, 2), (3, 4)):
        cand.append(jnp.where(top, v1[a0:a0 + 1], v1[a1:a1 + 1]) + v2h)
        eid.append(jnp.where(top, i1[a0:a0 + 1], i1[a1:a1 + 1]) * N_KEYS + i2h)
        order.append(jnp.where(top, a0 * PEER_TOPK, a1 * PEER_TOPK) + low)
    listed = rowf < PAIR_ROWS_LISTED
    cand.append(jnp.where(listed, -jnp.inf, v1 + v2[0:1]))
    eid.append(i1 * N_KEYS + i2[0:1])
    order.append(jnp.where(listed, BIG_ORDER, rowf * PEER_TOPK))
    listed = low < PAIR_ROWS_LISTED
    cand.append(jnp.where(listed, -jnp.inf, v1h + jnp.where(top, v2[1:2], v2[2:3])))
    eid.append(i1h * N_KEYS + jnp.where(top, i2[1:2], i2[2:3]))
    order.append(jnp.where(listed, BIG_ORDER, low * PEER_TOPK + jnp.where(top, 1.0, 2.0)))
    return jnp.concatenate(cand, axis=0), jnp.concatenate(order, axis=0), jnp.concatenate(eid, axis=0)


def _peer_topk_kernel(xf_ref, wpq_ref, sk1_ref, sk2_ref, eid_ref, gate_ref, *, tm):
    q = _dot(xf_ref[...].astype(BF16), wpq_ref[...])
    key_order = lax.broadcasted_iota(I32, (N_KEYS, tm), 0).astype(F32)
    for h in range(PEER_HEADS):
        qa = q[:, h * PEER_QDIM:h * PEER_QDIM + PEER_HALF].astype(BF16)
        qb = q[:, h * PEER_QDIM + PEER_HALF:(h + 1) * PEER_QDIM].astype(BF16)
        v1, i1 = _top16(_dot_nt(sk1_ref[...], qa), key_order, None)
        v2, i2 = _top16(_dot_nt(sk2_ref[...], qb), key_order, None)
        vals, eid = _top16(*_pair_candidates(v1, i1, v2, i2))
        e = jnp.exp(vals - vals[0:1])
        rows = slice(h * PEER_TOPK, (h + 1) * PEER_TOPK)
        gate_ref[rows, :] = e / jnp.sum(e, axis=0, keepdims=True)
        eid_ref[rows, :] = eid.astype(I32)


def _peer_topk(xf, w_pq, sk1, sk2, tm):
    n = xf.shape[0]
    out = pl.BlockSpec((PEER_PAIRS, tm), lambda i: (0, i))
    return pl.pallas_call(
        functools.partial(_peer_topk_kernel, tm=tm),
        grid=(n // tm,),
        in_specs=[pl.BlockSpec((tm, D_MODEL), lambda i: (i, 0)), _full(w_pq.shape),
                  _full(sk1.shape), _full(sk2.shape)],
        out_specs=[out, out],
        out_shape=[jax.ShapeDtypeStruct((PEER_PAIRS, n), I32), jax.ShapeDtypeStruct((PEER_PAIRS, n), F32)],
        compiler_params=_cparams(1), name="peer_topk",
    )(xf, w_pq, sk1, sk2)


def _peer_pack_kernel(pu_ref, pv_ref, tab_ref, *, tr):
    _pack_rows(pu_ref, pv_ref, tab_ref, tr)


def _peer_pack(peer_u, peer_v):
    ne = peer_u.shape[0]
    tr = 256
    return pl.pallas_call(
        functools.partial(_peer_pack_kernel, tr=tr),
        grid=(ne // tr,),
        in_specs=[pl.BlockSpec((tr, D_MODEL), lambda i: (i, 0))] * 2,
        out_specs=pl.BlockSpec((tr * ROW_CHUNKS, LANES), lambda i: (i, 0)),
        out_shape=jax.ShapeDtypeStruct((ne * ROW_CHUNKS, LANES), WORD),
        compiler_params=_cparams(1), name="peer_pack",
    )(peer_u, peer_v)


def _unpack(words):
    return tuple(pltpu.unpack_elementwise(words, index=i, packed_dtype=BF16, unpacked_dtype=F32) for i in (0, 1))


def _peer_mix_kernel(ids_ref, gate_ref, xf_ref, x1_ref, gfin_ref, tab_ref, y_ref, *scratch, tt):
    bufs, sems = scratch[:MIX_SLOTS], scratch[MIX_SLOTS]
    half_chunks = ROW_CHUNKS // 2
    step, last = pl.program_id(0), pl.num_programs(0) - 1

    def issue(t, k):
        for p in range(PEER_PAIRS):
            row0 = pl.multiple_of(ids_ref[0, t, p] * ROW_CHUNKS, ROW_CHUNKS)
            pltpu.make_async_copy(tab_ref.at[pl.ds(row0, ROW_CHUNKS)],
                                  bufs[k].at[pl.ds(p * ROW_PITCH, ROW_CHUNKS)],
                                  sems.at[k]).start(priority=p % 2)

    def wait(k):
        n = PEER_PAIRS * ROW_CHUNKS
        pltpu.make_async_copy(tab_ref.at[pl.ds(0, n)], bufs[k].at[pl.ds(0, n)], sems.at[k]).wait()

    def words(k, c):
        return bufs[k][pl.ds(c, PEER_PAIRS, stride=ROW_PITCH), :]

    pair_of_lane = lax.broadcasted_iota(I32, (PEER_PAIRS, 2 * PEER_PAIRS), 1) >> 1
    own_lane = pair_of_lane == lax.broadcasted_iota(I32, (PEER_PAIRS, 2 * PEER_PAIRS), 0)
    sub = lax.broadcasted_iota(I32, (8, 2 * PEER_PAIRS), 0)
    lane = lax.broadcasted_iota(I32, (8, 2 * PEER_PAIRS), 1)
    keep = (sub < 4) & ((lane & 1) == (sub >> 1))
    high_part = (sub & 1) == 0

    def gate_weights(k, xrow, grow):
        acc = jnp.zeros((PEER_PAIRS, LANES), F32)
        for c in range(half_chunks):
            lo, hi = _unpack(words(k, c))
            acc = acc + lo * xrow[:, c * LANES:(c + 1) * LANES]
            acc = acc + hi * xrow[:, HALF_D + c * LANES:HALF_D + (c + 1) * LANES]
        h = jnp.sum(acc, axis=-1, keepdims=True)
        hrow = jnp.sum(jnp.where(own_lane, h, 0.0), axis=0, keepdims=True)
        w = grow * _gelu(hrow)
        w_hi = w.astype(BF16).astype(F32)
        return jnp.where(keep, jnp.where(high_part, w_hi, w - w_hi), 0.0).astype(BF16)

    def mix_values(k, lhs):
        rhs = jnp.concatenate([pltpu.bitcast(words(k, half_chunks + c), BF16) for c in range(half_chunks)], axis=-1)
        o = _dot(lhs, rhs)
        return jnp.concatenate([o[0:1] + o[1:2], o[2:3] + o[3:4]], axis=-1)

    @pl.when(step == 0)
    def _():
        for k in range(MIX_AHEAD):
            issue(k, k)

    def group(g, carry):
        rows = pl.ds(pl.multiple_of(g * MIX_SLOTS, MIX_SLOTS), MIX_SLOTS)
        x_tile, g_tile = xf_ref[rows, :], gate_ref[rows, :]
        lhs_prev, yrows = None, []
        for k in range(MIX_SLOTS):
            wait(k)
            issue(g * MIX_SLOTS + k + MIX_AHEAD, (k + MIX_AHEAD) % MIX_SLOTS)
            lhs = gate_weights(k, x_tile[k:k + 1], g_tile[k:k + 1])
            if lhs_prev is not None:
                yrows.append(mix_values(k - 1, lhs_prev))
            lhs_prev = lhs
        yrows.append(mix_values(MIX_SLOTS - 1, lhs_prev))
        y_ref[rows, :] = _rms(x1_ref[rows, :] + jnp.concatenate(yrows, axis=0), gfin_ref[...])
        return carry

    lax.fori_loop(0, tt // MIX_SLOTS, group, 0)

    @pl.when(step == last)
    def _():
        for k in range(MIX_AHEAD):
            wait(k)


def _peer_mix(eid_t, gate_t, xf, x1, g_final, table, tt):
    n = xf.shape[0]
    steps = n // tt
    eid = eid_t.T
    look = jnp.concatenate([eid[tt:], eid[-tt:]], axis=0).reshape(steps, tt, PEER_PAIRS)[:, :MIX_SLOTS]
    ids = jnp.concatenate([eid.reshape(steps, tt, PEER_PAIRS), look], axis=1)
    gate = jnp.repeat(gate_t.T, 2, axis=1)
    row = pl.BlockSpec((tt, D_MODEL), lambda i: (i, 0))
    assert MIX_SLOTS == 8 and tt % MIX_SLOTS == 0
    return pl.pallas_call(
        functools.partial(_peer_mix_kernel, tt=tt),
        grid=(steps,),
        in_specs=[pl.BlockSpec((1, tt + MIX_SLOTS, PEER_PAIRS), lambda i: (i, 0, 0), memory_space=pltpu.SMEM),
                  pl.BlockSpec((tt, 2 * PEER_PAIRS), lambda i: (i, 0)),
                  row, row, _full((1, D_MODEL)),
                  pl.BlockSpec(memory_space=pl.ANY)],
        out_specs=row,
        out_shape=jax.ShapeDtypeStruct((n, D_MODEL), F32),
        scratch_shapes=[pltpu.VMEM((PEER_PAIRS * ROW_PITCH, LANES), WORD)] * MIX_SLOTS
        + [pltpu.SemaphoreType.DMA((MIX_SLOTS,))],
        compiler_params=_cparams(1), name="peer_mix",
    )(ids, gate, xf, x1, g_final, table)


def _rope_tables(pos):
    half = QK_ROPE // 2
    inv = ROPE_THETA ** (-jnp.arange(half, dtype=F32) / half)
    ang = pos.astype(F32)[:, None] * inv[None, :]
    cos, sin = jnp.cos(ang), jnp.sin(ang)
    return jnp.concatenate([cos, cos], -1), jnp.concatenate([sin, sin], -1)


def _rot_cols(w):
    half = w.shape[-1] // 2
    return jnp.concatenate([-w[..., half:], w[..., :half]], axis=-1)


def _group(x, tokens_per_seq, pos, cache, mem_kv, wts, g_final):
    n = x.shape[0]
    nseq = n // tokens_per_seq
    tm = min(256, n)
    cos2, sin2 = _rope_tables(pos)
    reps = max(tm // tokens_per_seq, 1)
    cos2, sin2 = jnp.tile(cos2, (reps, 1)), jnp.tile(sin2, (reps, 1))
    cos8, sin8 = jnp.tile(cos2, (1, MLA_HEADS)), jnp.tile(sin2, (1, MLA_HEADS))

    ya, v, ckv, krope, kcat, q_m, qcat = _front(x, wts, cos2, sin2, cos8, sin8, tm, min(GMLP_CHUNK, tokens_per_seq))
    if cache is None:
        yb, table = _mla_attn(qcat, kcat, jnp.swapaxes(wts["w_uvt"], 1, 2), wts["g_out_b"].reshape(MLA_DIM, 1),
                              wts["peer_u"], wts["peer_v"], nseq, tokens_per_seq, min(256, tokens_per_seq))
    else:
        yb = _mla_dec(qcat, cache[0], cache[1], kcat, wts["w_uvt"], wts["g_out_b"], nseq, tokens_per_seq)
        table = wts["table"]
    mk, mv = mem_kv
    x1, xf = _merge(x, ya, yb, q_m, mk, mv, wts["g_out_m"], wts["w_out"], wts["g_ffn"], min(MERGE_ROWS, n),
                    tokens_per_seq)
    eid_t, gate_t = _peer_topk(xf, wts["w_pq"], wts["sk1"], wts["sk2"], tm)
    y = _peer_mix(eid_t, gate_t, xf, x1, g_final, table, min(64, n))
    return y, ckv, krope, v, table


def kernel(x_prompt, x_sample, cache_mla_ckv, cache_mla_krope, cache_mem_k, cache_mem_v, mem_prompt, g_attn, w_in, g_v, b_v, w_s, b_s, g_q, w_uq, w_uk, w_uv, g_kv, g_mem, w_mk, w_mv, g_out_a, g_out_b, g_out_m, w_out, g_ffn, w_pq, sub_keys1, sub_keys2, peer_u, peer_v, g_final):
    assert w_in.shape[0] == 1, "the final norm is fused after the single layer"
    l = 0
    bp, sp, _ = x_prompt.shape
    bs, ts, _ = x_sample.shape
    past = cache_mla_ckv.shape[2]
    gfin = g_final.reshape(1, D_MODEL)
    wi = w_in[l]
    w_ext = jnp.concatenate([wi[:, :_C_KROT], _rot_cols(wi[:, _C_KR:_C_KROT]), wi[:, _C_KROT:]], axis=1)
    wq = w_uq[l].reshape(Q_LORA, MLA_HEADS, QK_NOPE + QK_ROPE)
    wq_rope = wq[:, :, QK_NOPE:]
    w_uqp = jnp.concatenate([wq[:, :, :QK_NOPE].reshape(Q_LORA, -1), wq_rope.reshape(Q_LORA, -1),
                             _rot_cols(wq_rope).reshape(Q_LORA, -1)], axis=1)
    wts = {
        "g_attn": g_attn[l].reshape(1, -1), "w_ext": w_ext.astype(BF16), "g_kv": g_kv[l].reshape(1, -1),
        "g_v": g_v[l].reshape(1, -1), "b_v": b_v[l].reshape(1, -1), "w_s": w_s[l], "b_st": b_s[l].T,
        "g_out_a": g_out_a[l].reshape(1, -1), "g_q": g_q[l].reshape(1, -1), "w_uqp": w_uqp.astype(BF16),
        "w_ukt": jnp.transpose(w_uk[l], (1, 2, 0)).astype(BF16),
        "w_uvt": jnp.transpose(w_uv[l], (1, 0, 2)).astype(BF16),
        "g_out_b": g_out_b[l].reshape(1, -1), "g_out_m": g_out_m[l].reshape(1, -1),
        "w_out": w_out[l].astype(BF16), "g_ffn": g_ffn[l].reshape(1, -1), "w_pq": w_pq[l].astype(BF16),
        "sk1": sub_keys1[l].astype(BF16), "sk2": sub_keys2[l].astype(BF16),
        "peer_u": peer_u[l], "peer_v": peer_v[l],
    }
    mk, mv = _memkv(mem_prompt.reshape(bp * MEM_TOKENS, D_MODEL), g_mem[l].reshape(1, -1),
                    w_mk[l].astype(BF16), w_mv[l].astype(BF16))
    yp, ckv_p, kr_p, _, wts["table"] = _group(x_prompt.reshape(bp * sp, D_MODEL), sp, jnp.arange(sp), None, (mk, mv),
                                              wts, gfin)
    mem_s = (cache_mem_k[l].reshape(bs * MEM_TOKENS, MEM_DIM), cache_mem_v[l].reshape(bs * MEM_TOKENS, MEM_DIM))
    ys, ckv_s, kr_s, gv_s, _ = _group(x_sample.reshape(bs * ts, D_MODEL), ts, past + jnp.arange(ts),
                                      (cache_mla_ckv[l], cache_mla_krope[l]), mem_s, wts, gfin)
    return (yp.reshape(bp, sp, D_MODEL), ys.reshape(bs, ts, D_MODEL),
            ckv_p.reshape(1, bp, sp, KV_LORA), kr_p.reshape(1, bp, sp, QK_ROPE),
            mk.reshape(1, bp, MEM_TOKENS, MEM_HEADS, MEM_HEAD_DIM),
            mv.reshape(1, bp, MEM_TOKENS, MEM_HEADS, MEM_HEAD_DIM),
            ckv_s.reshape(1, bs, ts, KV_LORA), kr_s.reshape(1, bs, ts, QK_ROPE),
            gv_s.reshape(1, bs, ts, GMLP_DIM))
```
